```python
import jax, jax.numpy as jnp
from jax import lax
import numpy as np

D_MODEL = 1024
BATCH = 8
SEQ = 4096
DEPTH = 2

N_MIXERS = 2
ATTN_PAIRS = ((128, 1), (512, 4), (2048, 16))
N_ATTN_GROUPS = len(ATTN_PAIRS)
HEADS_PER_GROUP = 8
HEAD_DIM = D_MODEL // HEADS_PER_GROUP
ATTN_WIDTH = HEADS_PER_GROUP * HEAD_DIM
QKV_WIDTH = N_ATTN_GROUPS * 3 * ATTN_WIDTH
N_ALIBI_HEADS = N_ATTN_GROUPS * HEADS_PER_GROUP
Q_BLOCK = 128
POOL_WINDOWS = (2, 4, 8, 16)
POOL_GROUPS = len(POOL_WINDOWS)
POOL_GROUP_DIM = D_MODEL // POOL_GROUPS
D_FF = ((8 * D_MODEL + 3 * 256 - 1) // (3 * 256)) * 256
N_ATTN_LAYERS = (DEPTH + 1) // 2
N_POOL_LAYERS = DEPTH // 2
RMS_EPS = 1e-6

kernel_name = "dilated_attn_pool_hybrid_trunk"


def rmsnorm(x, g):
    xf = x.astype(jnp.float32)
    y = xf * lax.rsqrt(jnp.mean(xf * xf, axis=-1, keepdims=True) + RMS_EPS)
    return (y * g.astype(jnp.float32)).astype(x.dtype)


def alibi_slopes():
    n = N_ALIBI_HEADS
    return jnp.exp2(-8.0 * jnp.arange(1, n + 1, dtype=jnp.float32) / n)


def dilated_window_attention(q, k, v, window, dil, slopes):
    B, S, H, D = q.shape
    L = S // dil
    w_sub = window // dil
    nb = -(-L // Q_BLOCK)
    Lp = nb * Q_BLOCK
    pad = Lp - L
    Bd = B * dil

    def fold(a):
        return a.reshape(B, L, dil, H, D).transpose(0, 2, 1, 3, 4).reshape(Bd, L, H, D)

    def band(a):
        a = jnp.pad(a, ((0, 0), (Q_BLOCK, pad), (0, 0), (0, 0))).reshape(Bd, nb + 1, Q_BLOCK, H, D)
        return jnp.concatenate([a[:, :-1], a[:, 1:]], axis=2)

    qb = jnp.pad(fold(q), ((0, 0), (0, pad), (0, 0), (0, 0))).reshape(Bd, nb, Q_BLOCK, H, D)
    kb = band(fold(k))
    vb = band(fold(v))

    s = jnp.einsum('bnqhd,bnkhd->bnhqk', qb, kb) * (D ** -0.5)
    qi = jnp.arange(Q_BLOCK)[:, None]
    ki = jnp.arange(2 * Q_BLOCK)[None, :]
    delta = Q_BLOCK + qi - ki
    blk = jnp.arange(nb)[:, None, None]
    key_idx = blk * Q_BLOCK + ki[None] - Q_BLOCK
    valid = (delta >= 0)[None] & (delta <= w_sub)[None] & (key_idx >= 0)
    bias = -slopes[:, None, None] * (delta * dil).astype(jnp.float32)[None]
    s = s + bias[None, None]
    s = jnp.where(valid[None, :, None], s, -jnp.inf)
    m = jnp.max(s, axis=-1, keepdims=True)
    p = jnp.exp(s - m)
    den = jnp.sum(p, axis=-1, keepdims=True)
    o = jnp.einsum('bnhqk,bnkhd->bnqhd', p / den, vb)
    lse = (m + jnp.log(den))[..., 0].transpose(0, 1, 3, 2)

    o = o.reshape(Bd, Lp, H, D)[:, :L]
    o = o.reshape(B, dil, L, H, D).transpose(0, 2, 1, 3, 4).reshape(B, S, H, D)
    lse = lse.reshape(Bd, Lp, H)[:, :L]
    lse = lse.reshape(B, dil, L, H).transpose(0, 2, 1, 3).reshape(B, S, H)
    return o, lse


def dilated_attention_mixer(h, w_qkv, w_out):
    B, S, _ = h.shape
    qkv = (h @ w_qkv).astype(jnp.float32).reshape(B, S, N_ATTN_GROUPS, 3, HEADS_PER_GROUP, HEAD_DIM)
    slopes = alibi_slopes().reshape(N_ATTN_GROUPS, HEADS_PER_GROUP)
    outs, lses = [], []
    for g, (window, dil) in enumerate(ATTN_PAIRS):
        o, lse = dilated_window_attention(qkv[:, :, g, 0], qkv[:, :, g, 1], qkv[:, :, g, 2], window, dil, slopes[g])
        outs.append(o)
        lses.append(lse)
    o = jnp.stack(outs, axis=0)
    wts = jax.nn.softmax(jnp.stack(lses, axis=0), axis=0)
    o = jnp.sum(wts[..., None] * o, axis=0).reshape(B, S, ATTN_WIDTH)
    return o.astype(h.dtype) @ w_out


def trailing_mean(u, w):
    B, S, C = u.shape
    c = jnp.cumsum(u, axis=1)
    shifted = jnp.concatenate([jnp.zeros((B, w, C), u.dtype), c[:, :S - w]], axis=1)
    count = jnp.minimum(jnp.arange(1, S + 1), w).astype(jnp.float32)[None, :, None]
    return (c - shifted) / count


def pooling_mixer(h, w_in, w_group, scale):
    B, S, _ = h.shape
    u = (h @ w_in).astype(jnp.float32).reshape(B, S, POOL_GROUPS, POOL_GROUP_DIM)
    ys = [trailing_mean(u[:, :, g], w) - u[:, :, g] for g, w in enumerate(POOL_WINDOWS)]
    y = jnp.stack(ys, axis=2)
    y = jnp.einsum('bsgc,gcd->bsgd', y, w_group.astype(jnp.float32)).reshape(B, S, D_MODEL)
    return (y * scale.astype(jnp.float32)).astype(h.dtype)


def swiglu_ffn(h, w_gate_up, w_down):
    gu = h @ w_gate_up
    gate, up = gu[..., :D_FF], gu[..., D_FF:]
    return (jax.nn.silu(gate) * up) @ w_down


def setup_inputs(seed: int = 0) -> dict:
    key = jax.random.key(seed)
    ks = jax.random.split(key, 13)
    f32 = jnp.float32
    nA, nP = N_ATTN_LAYERS, N_POOL_LAYERS
    return {
        "x": jax.random.normal(ks[0], (BATCH, SEQ, D_MODEL), f32),
        "attn_norm": 1.0 + 0.02 * jax.random.normal(ks[1], (nA, D_MODEL), f32),
        "w_qkv": jax.random.normal(ks[2], (nA, D_MODEL, QKV_WIDTH), f32) * D_MODEL ** -0.5,
        "w_attn_out": jax.random.normal(ks[3], (nA, ATTN_WIDTH, D_MODEL), f32) * ATTN_WIDTH ** -0.5,
        "pool_norm": 1.0 + 0.02 * jax.random.normal(ks[4], (nP, D_MODEL), f32),
        "w_pool_in": jax.random.normal(ks[5], (nP, D_MODEL, D_MODEL), f32) * D_MODEL ** -0.5,
        "w_pool_group": jax.random.normal(ks[6], (nP, POOL_GROUPS, POOL_GROUP_DIM, POOL_GROUP_DIM), f32) * POOL_GROUP_DIM ** -0.5,
        "pool_scale": 0.5 + 0.1 * jax.random.normal(ks[7], (nP, D_MODEL), f32),
        "ffn_norm": 1.0 + 0.02 * jax.random.normal(ks[8], (DEPTH, D_MODEL), f32),
        "w_ffn_gate_up": jax.random.normal(ks[9], (DEPTH, D_MODEL, 2 * D_FF), f32) * D_MODEL ** -0.5,
        "w_ffn_down": jax.random.normal(ks[10], (DEPTH, D_FF, D_MODEL), f32) * D_FF ** -0.5,
        "final_norm": 1.0 + 0.02 * jax.random.normal(ks[11], (D_MODEL,), f32),
    }


def reference(x, attn_norm, w_qkv, w_attn_out, pool_norm, w_pool_in, w_pool_group, pool_scale,
              ffn_norm, w_ffn_gate_up, w_ffn_down, final_norm):
    for i in range(DEPTH):
        j = i // N_MIXERS
        if i % N_MIXERS == 0:
            x = x + dilated_attention_mixer(rmsnorm(x, attn_norm[j]), w_qkv[j], w_attn_out[j])
        else:
            x = x + pooling_mixer(rmsnorm(x, pool_norm[j]), w_pool_in[j], w_pool_group[j], pool_scale[j])
        x = x + swiglu_ffn(rmsnorm(x, ffn_norm[i]), w_ffn_gate_up[i], w_ffn_down[i])
    return rmsnorm(x, final_norm)
```

```python
import functools

import jax
import jax.numpy as jnp
from jax import lax
from jax.experimental import pallas as pl
from jax.experimental.pallas import tpu as pltpu

F32 = jnp.float32
BF16 = jnp.bfloat16

D_MODEL = 1024
HEADS = 8
HEAD_DIM = 128
ATTN_PAIRS = ((128, 1), (512, 4), (2048, 16))
N_GROUPS = len(ATTN_PAIRS)
GROUP_QKV = 3 * D_MODEL
Q_BLOCK = 128
POOL_WINDOWS = (2, 4, 8, 16)
POOL_GROUP_DIM = D_MODEL // len(POOL_WINDOWS)
POOL_HALO = 16
D_FF = 2816
RMS_EPS = 1e-6
LANES = 128
N_CHUNKS = D_MODEL // LANES
LSE_LANES = LANES

VMEM_LIMIT_BYTES = 56 * 1024 * 1024

QKV_TM = 512
ATTN_TQ = 512
POST_TM = 256
POOL_TM = 256


def _rms(x, gamma):
    return x * lax.rsqrt(jnp.mean(x * x, axis=-1, keepdims=True) + RMS_EPS) * gamma


def _dot(a, b):
    return jnp.dot(a, b, preferred_element_type=F32)


def _ffn(x, gamma, wgu_ref, wd_ref):
    h = _rms(x, gamma).astype(BF16)
    gu = _dot(h, wgu_ref[...])
    gate, up = gu[:, :D_FF], gu[:, D_FF:]
    act = (gate * jax.nn.sigmoid(gate) * up).astype(BF16)
    return x + _dot(act, wd_ref[...])


def _resident(shape):
    return pl.BlockSpec(shape, lambda *_: (0,) * len(shape), pipeline_mode=pl.Buffered(1))


def _qkv_kernel(x_ref, g_ref, w_ref, o1_ref, o2_ref, o3_ref, h_scr, hp_scr):
    tm = x_ref.shape[0]
    h = _rms(x_ref[...], g_ref[...])
    for c in range(N_CHUNKS):
        h_scr[c] = h[:, c * LANES:(c + 1) * LANES]
    out_refs = (o1_ref, o2_ref, o3_ref)
    for g, (_, dil) in enumerate(ATTN_PAIRS):
        rows = tm // dil
        if dil == 1:
            hp = h.astype(BF16)
        else:
            for r in range(dil):
                for c in range(N_CHUNKS):
                    hp_scr[r * rows:(r + 1) * rows, c * LANES:(c + 1) * LANES] = (
                        h_scr[c, pl.ds(r, rows, stride=dil), :].astype(BF16))
            hp = hp_scr[...]
        y = _dot(hp, w_ref[:, g * GROUP_QKV:(g + 1) * GROUP_QKV]).astype(BF16)
        for r in range(dil):
            out_refs[g][r] = y[r * rows:(r + 1) * rows, :]


def _qkv_proj(x, gamma, w_qkv):
    B, S, D = x.shape
    tm = QKV_TM
    out_shape = [jax.ShapeDtypeStruct((B, dil, S // dil, GROUP_QKV), BF16) for _, dil in ATTN_PAIRS]
    out_specs = [pl.BlockSpec((None, dil, tm // dil, GROUP_QKV), lambda b, i: (b, 0, i, 0))
                 for _, dil in ATTN_PAIRS]
    return pl.pallas_call(
        _qkv_kernel,
        grid=(B, S // tm),
        in_specs=[
            pl.BlockSpec((None, tm, D), lambda b, i: (b, i, 0)),
            _resident((1, D)),
            _resident((D, N_GROUPS * GROUP_QKV)),
        ],
        out_specs=out_specs,
        out_shape=out_shape,
        scratch_shapes=[pltpu.VMEM((N_CHUNKS, tm, LANES), F32), pltpu.VMEM((tm, D), BF16)],
        compiler_params=pltpu.CompilerParams(
            dimension_semantics=("arbitrary", "arbitrary"), vmem_limit_bytes=VMEM_LIMIT_BYTES),
    )(x, gamma, w_qkv)


def _attn_kernel(q_ref, kh_ref, km_ref, vh_ref, vm_ref, o_ref, lse_ref, *, dil, slopes):
    tq = q_ref.shape[0]
    n = pl.program_id(2)
    qi = lax.broadcasted_iota(jnp.int32, (Q_BLOCK, 2 * Q_BLOCK), 0)
    ki = lax.broadcasted_iota(jnp.int32, (Q_BLOCK, 2 * Q_BLOCK), 1)
    delta = Q_BLOCK + qi - ki
    valid = (delta >= 0) & (delta <= Q_BLOCK)
    valid_first = valid & (ki >= jnp.where(n == 0, Q_BLOCK, 0))
    dist = (delta * dil).astype(F32)
    lane = lax.broadcasted_iota(jnp.int32, (Q_BLOCK, LSE_LANES), 1)
    scale = HEAD_DIM ** -0.5
    for i in range(tq // Q_BLOCK):
        rows = slice(i * Q_BLOCK, (i + 1) * Q_BLOCK)
        mask = valid_first if i == 0 else valid
        lse_tile = jnp.zeros((Q_BLOCK, LSE_LANES), F32)
        for h in range(HEADS):
            cols = slice(h * HEAD_DIM, (h + 1) * HEAD_DIM)
            q = q_ref[rows, cols]
            if i == 0:
                k = jnp.concatenate([kh_ref[:, cols], km_ref[0:Q_BLOCK, cols]], axis=0)
                v = jnp.concatenate([vh_ref[:, cols], vm_ref[0:Q_BLOCK, cols]], axis=0)
            else:
                k = km_ref[(i - 1) * Q_BLOCK:(i + 1) * Q_BLOCK, cols]
                v = vm_ref[(i - 1) * Q_BLOCK:(i + 1) * Q_BLOCK, cols]
            s = lax.dot_general(q, k, (((1,), (1,)), ((), ())), preferred_element_type=F32)
            s = s * scale - slopes[h] * dist
            s = jnp.where(mask, s, -jnp.inf)
            m = jnp.max(s, axis=-1, keepdims=True)
            p = jnp.exp(s - m)
            den = jnp.sum(p, axis=-1, keepdims=True)
            o = _dot(p.astype(BF16), v) / den
            o_ref[rows, cols] = o.astype(o_ref.dtype)
            lse_tile = jnp.where(lane == h, m + jnp.log(den), lse_tile)
        lse_ref[rows, :] = lse_tile


def _attention(qkv, group):
    _, dil = ATTN_PAIRS[group]
    B, _, L, _ = qkv.shape
    tq = min(ATTN_TQ, L)
    sub = tq // Q_BLOCK
    n_heads = N_GROUPS * HEADS
    slopes = tuple(2.0 ** (-8.0 * (group * HEADS + h + 1) / n_heads) for h in range(HEADS))

    def main(c):
        return pl.BlockSpec((None, None, tq, D_MODEL), lambda b, r, n: (b, r, n, c))

    def halo(c):
        return pl.BlockSpec((None, None, Q_BLOCK, D_MODEL),
                            lambda b, r, n: (b, r, jnp.maximum(n * sub - 1, 0), c))

    return pl.pallas_call(
        functools.partial(_attn_kernel, dil=dil, slopes=slopes),
        grid=(B, dil, L // tq),
        in_specs=[main(0), halo(1), main(1), halo(2), main(2)],
        out_specs=[
            pl.BlockSpec((None, None, tq, D_MODEL), lambda b, r, n: (b, r, n, 0)),
            pl.BlockSpec((None, None, tq, LSE_LANES), lambda b, r, n: (b, r, n, 0)),
        ],
        out_shape=[
            jax.ShapeDtypeStruct((B, dil, L, D_MODEL), BF16),
            jax.ShapeDtypeStruct((B, dil, L, LSE_LANES), F32),
        ],
        compiler_params=pltpu.CompilerParams(
            dimension_semantics=("arbitrary", "arbitrary", "arbitrary"),
            vmem_limit_bytes=VMEM_LIMIT_BYTES),
    )(qkv, qkv, qkv, qkv, qkv)


def _post_attn_kernel(x_ref, o1_ref, o2_ref, o3_ref, l1_ref, l2_ref, l3_ref, wo_ref,
                      gf_ref, wgu_ref, wd_ref, out_ref, o_scr, l_scr):
    tm = x_ref.shape[0]
    o_refs = (o1_ref, o2_ref, o3_ref)
    l_refs = (l1_ref, l2_ref, l3_ref)
    for g, (_, dil) in enumerate(ATTN_PAIRS):
        rows = tm // dil
        for r in range(dil):
            l_scr[g, pl.ds(r, rows, stride=dil), :] = l_refs[g][r]
            for h in range(HEADS):
                o_scr[g, h, pl.ds(r, rows, stride=dil), :] = (
                    o_refs[g][r, :, h * HEAD_DIM:(h + 1) * HEAD_DIM].astype(F32))
    lse = [l_scr[g] for g in range(N_GROUPS)]
    m = jnp.maximum(jnp.maximum(lse[0], lse[1]), lse[2])
    e = [jnp.exp(l - m) for l in lse]
    inv = 1.0 / (e[0] + e[1] + e[2])
    wts = [ei * inv for ei in e]
    heads = []
    for h in range(HEADS):
        acc = wts[0][:, h:h + 1] * o_scr[0, h]
        for g in range(1, N_GROUPS):
            acc = acc + wts[g][:, h:h + 1] * o_scr[g, h]
        heads.append(acc.astype(BF16))
    merged = jnp.concatenate(heads, axis=-1)
    x1 = x_ref[...] + _dot(merged, wo_ref[...])
    out_ref[...] = _ffn(x1, gf_ref[...], wgu_ref, wd_ref)


def _post_attn(x, outs, lses, w_out, g_ffn, w_gu, w_down):
    B, S, D = x.shape
    tm = POST_TM
    tile = pl.BlockSpec((None, tm, D), lambda b, i: (b, i, 0))
    o_specs = [pl.BlockSpec((None, dil, tm // dil, D), lambda b, i: (b, 0, i, 0)) for _, dil in ATTN_PAIRS]
    l_specs = [pl.BlockSpec((None, dil, tm // dil, LSE_LANES), lambda b, i: (b, 0, i, 0))
               for _, dil in ATTN_PAIRS]
    return pl.pallas_call(
        _post_attn_kernel,
        grid=(B, S // tm),
        in_specs=[tile, *o_specs, *l_specs, _resident((D, D)), _resident((1, D)),
                  _resident((D, 2 * D_FF)), _resident((D_FF, D))],
        out_specs=tile,
        out_shape=jax.ShapeDtypeStruct((B, S, D), F32),
        scratch_shapes=[pltpu.VMEM((N_GROUPS, HEADS, tm, HEAD_DIM), F32),
                        pltpu.VMEM((N_GROUPS, tm, LSE_LANES), F32)],
        compiler_params=pltpu.CompilerParams(
            dimension_semantics=("arbitrary", "arbitrary"), vmem_limit_bytes=VMEM_LIMIT_BYTES),
    )(x, *outs, *lses, w_out, g_ffn, w_gu, w_down)


def _pool_ffn_kernel(x_ref, gp_ref, win_ref, wg_ref, sc_ref, gf_ref, wgu_ref, wd_ref, gn_ref,
                     out_ref, u_scr):
    tm = x_ref.shape[0]
    i = pl.program_id(1)
    x = x_ref[...]
    u = _dot(_rms(x, gp_ref[...]).astype(BF16), win_ref[...])

    @pl.when(i == 0)
    def _():
        u_scr[0:POOL_HALO, :] = jnp.zeros((POOL_HALO, D_MODEL), F32)

    @pl.when(i > 0)
    def _():
        u_scr[0:POOL_HALO, :] = u_scr[tm:tm + POOL_HALO, :]

    u_scr[POOL_HALO:, :] = u
    pos = i * tm + lax.broadcasted_iota(jnp.int32, (tm, 1), 0) + 1
    ys = []
    for g, w in enumerate(POOL_WINDOWS):
        cols = slice(g * POOL_GROUP_DIM, (g + 1) * POOL_GROUP_DIM)
        s = u_scr[:, cols]
        shift = 1
        while shift < w:
            s = s + pltpu.roll(s, shift, axis=0)
            shift *= 2
        count = jnp.minimum(pos, w).astype(F32)
        y = s[POOL_HALO:, :] / count - u[:, cols]
        ys.append(_dot(y.astype(BF16), wg_ref[g]))
    y = jnp.concatenate(ys, axis=-1) * sc_ref[...]
    x2 = x + y
    x3 = _ffn(x2, gf_ref[...], wgu_ref, wd_ref)
    out_ref[...] = _rms(x3, gn_ref[...])


def _pool_ffn(x, g_pool, w_in, w_group, scale, g_ffn, w_gu, w_down, g_final):
    B, S, D = x.shape
    tm = POOL_TM
    tile = pl.BlockSpec((None, tm, D), lambda b, i: (b, i, 0))
    return pl.pallas_call(
        _pool_ffn_kernel,
        grid=(B, S // tm),
        in_specs=[tile, _resident((1, D)), _resident((D, D)),
                  _resident((len(POOL_WINDOWS), POOL_GROUP_DIM, POOL_GROUP_DIM)),
                  _resident((1, D)), _resident((1, D)),
                  _resident((D, 2 * D_FF)), _resident((D_FF, D)), _resident((1, D))],
        out_specs=tile,
        out_shape=jax.ShapeDtypeStruct((B, S, D), F32),
        scratch_shapes=[pltpu.VMEM((POOL_HALO + tm, D), F32)],
        compiler_params=pltpu.CompilerParams(
            dimension_semantics=("arbitrary", "arbitrary"), vmem_limit_bytes=VMEM_LIMIT_BYTES),
    )(x, g_pool, w_in, w_group, scale, g_ffn, w_gu, w_down, g_final)


def kernel(x, attn_norm, w_qkv, w_attn_out, pool_norm, w_pool_in, w_pool_group, pool_scale,
           ffn_norm, w_ffn_gate_up, w_ffn_down, final_norm):
    assert attn_norm.shape[0] == 1 and pool_norm.shape[0] == 1 and ffn_norm.shape[0] == 2
    bf = lambda w: w.astype(BF16)
    row = lambda v: v.reshape(1, D_MODEL)

    qkvs = _qkv_proj(x, row(attn_norm[0]), bf(w_qkv[0]))
    outs, lses = zip(*[_attention(qkvs[g], g) for g in range(N_GROUPS)])
    x1 = _post_attn(x, outs, lses, bf(w_attn_out[0]), row(ffn_norm[0]),
                    bf(w_ffn_gate_up[0]), bf(w_ffn_down[0]))
    return _pool_ffn(x1, row(pool_norm[0]), bf(w_pool_in[0]), bf(w_pool_group[0]), row(pool_scale[0]),
                     row(ffn_norm[1]), bf(w_ffn_gate_up[1]), bf(w_ffn_down[1]), row(final_norm))
```

```python
import functools

import jax
import jax.numpy as jnp
from jax import lax
from jax.experimental import pallas as pl
from jax.experimental.pallas import tpu as pltpu

F32 = jnp.float32
BF16 = jnp.bfloat16

D_MODEL = 1024
HEADS = 8
HEAD_DIM = 128
ATTN_PAIRS = ((128, 1), (512, 4), (2048, 16))
N_GROUPS = len(ATTN_PAIRS)
GROUP_QKV = 3 * D_MODEL
Q_BLOCK = 128
POOL_WINDOWS = (2, 4, 8, 16)
POOL_GROUP_DIM = D_MODEL // len(POOL_WINDOWS)
POOL_HALO = 16
D_FF = 2816
RMS_EPS = 1e-6
LOG2_E = 1.4426950408889634
LN_2 = 0.6931471805599453
Q_SCALE = LOG2_E * HEAD_DIM ** -0.5
LANES = 128
ROW_PACK = 2
N_CHUNKS = D_MODEL // LANES
LSE_LANES = LANES

VMEM_LIMIT_BYTES = 56 * 1024 * 1024

QKV_TM = 512
ATTN_TQ = 512
POST_TM = 256
POOL_TM = 256


def _rms(x, gamma):
    return x * lax.rsqrt(jnp.mean(x * x, axis=-1, keepdims=True) + RMS_EPS) * gamma


def _dot(a, b):
    return jnp.dot(a, b, preferred_element_type=F32)


def _ffn(x, gamma, wgu_ref, wd_ref):
    h = _rms(x, gamma).astype(BF16)
    gu = _dot(h, wgu_ref[...])
    gate, up = gu[:, :D_FF], gu[:, D_FF:]
    act = (gate * jax.nn.sigmoid(gate) * up).astype(BF16)
    return x + _dot(act, wd_ref[...])


def _pack_rows(x):
    return pltpu.bitcast(x, jnp.int32)


def _unpack_rows(x):
    return pltpu.bitcast(x, BF16)


def _resident(shape):
    return pl.BlockSpec(shape, lambda *_: (0,) * len(shape), pipeline_mode=pl.Buffered(1))


def _qkv_kernel(x_ref, g_ref, w_ref, o1_ref, o2_ref, o3_ref, h_scr, hp_scr):
    tm = x_ref.shape[0]
    h = _rms(x_ref[...], g_ref[...])
    for c in range(N_CHUNKS):
        h_scr[c] = h[:, c * LANES:(c + 1) * LANES]
    out_refs = (o1_ref, o2_ref, o3_ref)
    for g, (_, dil) in enumerate(ATTN_PAIRS):
        rows = tm // dil
        if dil == 1:
            hp = h.astype(BF16)
        else:
            for r in range(dil):
                for c in range(N_CHUNKS):
                    hp_scr[r * rows:(r + 1) * rows, c * LANES:(c + 1) * LANES] = (
                        h_scr[c, pl.ds(r, rows, stride=dil), :].astype(BF16))
            hp = hp_scr[...]
        y = _dot(hp, w_ref[:, g * GROUP_QKV:(g + 1) * GROUP_QKV])
        yq = (y[:, :D_MODEL] * Q_SCALE).astype(BF16)
        ykv = y[:, D_MODEL:].astype(BF16)
        for r in range(dil):
            out_refs[g][r, :, :D_MODEL] = _pack_rows(yq[r * rows:(r + 1) * rows, :])
            out_refs[g][r, :, D_MODEL:] = _pack_rows(ykv[r * rows:(r + 1) * rows, :])


def _qkv_proj(x, gamma, w_qkv):
    B, S, D = x.shape
    tm = QKV_TM
    out_shape = [jax.ShapeDtypeStruct((B, dil, S // dil // ROW_PACK, GROUP_QKV), jnp.int32)
                 for _, dil in ATTN_PAIRS]
    out_specs = [pl.BlockSpec((None, dil, tm // dil // ROW_PACK, GROUP_QKV), lambda b, i: (b, 0, i, 0))
                 for _, dil in ATTN_PAIRS]
    return pl.pallas_call(
        _qkv_kernel,
        name="qkv_proj",
        grid=(B, S // tm),
        in_specs=[
            pl.BlockSpec((None, tm, D), lambda b, i: (b, i, 0)),
            _resident((1, D)),
            _resident((D, N_GROUPS * GROUP_QKV)),
        ],
        out_specs=out_specs,
        out_shape=out_shape,
        scratch_shapes=[pltpu.VMEM((N_CHUNKS, tm, LANES), F32), pltpu.VMEM((tm, D), BF16)],
        compiler_params=pltpu.CompilerParams(
            dimension_semantics=("arbitrary", "arbitrary"), vmem_limit_bytes=VMEM_LIMIT_BYTES),
    )(x, gamma, w_qkv)


def _attn_kernel(q_ref, kh_ref, km_ref, vh_ref, vm_ref, o_ref, lse_ref, bias_scr, *, dil, slopes):
    tq = lse_ref.shape[0]
    n = pl.program_id(2)
    ki = lax.broadcasted_iota(jnp.int32, (Q_BLOCK, 2 * Q_BLOCK), 1)

    @pl.when((pl.program_id(0) == 0) & (pl.program_id(1) == 0) & (n == 0))
    def _():
        qi = lax.broadcasted_iota(jnp.int32, (Q_BLOCK, 2 * Q_BLOCK), 0)
        delta = Q_BLOCK + qi - ki
        valid = (delta >= 0) & (delta <= Q_BLOCK)
        dist = (delta * dil).astype(F32)
        for h in range(HEADS):
            bias_scr[h] = jnp.where(valid, (-slopes[h] * LOG2_E) * dist, -jnp.inf)

    first_lo = jnp.where(n == 0, Q_BLOCK, 0)
    lane = lax.broadcasted_iota(jnp.int32, (Q_BLOCK, LSE_LANES), 1)
    ones = jnp.ones((2 * Q_BLOCK, HEAD_DIM), BF16)
    pb = Q_BLOCK // ROW_PACK
    for i in range(tq // Q_BLOCK):
        rows = slice(i * Q_BLOCK, (i + 1) * Q_BLOCK)
        prows = slice(i * pb, (i + 1) * pb)
        m_tile = jnp.zeros((Q_BLOCK, LSE_LANES), F32)
        den_tile = jnp.ones((Q_BLOCK, LSE_LANES), F32)
        for h in range(HEADS):
            cols = slice(h * HEAD_DIM, (h + 1) * HEAD_DIM)
            q = _unpack_rows(q_ref[prows, cols])
            if i == 0:
                k = _unpack_rows(jnp.concatenate([kh_ref[:, cols], km_ref[0:pb, cols]], axis=0))
                v = _unpack_rows(jnp.concatenate([vh_ref[:, cols], vm_ref[0:pb, cols]], axis=0))
            else:
                k = _unpack_rows(km_ref[(i - 1) * pb:(i + 1) * pb, cols])
                v = _unpack_rows(vm_ref[(i - 1) * pb:(i + 1) * pb, cols])
            t = lax.dot_general(q, k, (((1,), (1,)), ((), ())), preferred_element_type=F32)
            t = t + bias_scr[h]
            if i == 0:
                t = jnp.where(ki >= first_lo, t, -jnp.inf)
            m = jnp.max(t, axis=-1, keepdims=True)
            p = jnp.exp2(t - m).astype(BF16)
            ov = _dot(p, jnp.concatenate([v, ones], axis=1))
            den = ov[:, HEAD_DIM:]
            o_ref[prows, cols] = _pack_rows((ov[:, :HEAD_DIM] / den).astype(BF16))
            m_tile = jnp.where(lane == h, m, m_tile)
            den_tile = jnp.where(lane == h, den, den_tile)
        lse_ref[rows, :] = (m_tile + jnp.log2(den_tile)) * LN_2


def _attention(qkv, group):
    _, dil = ATTN_PAIRS[group]
    B, _, Lp, _ = qkv.shape
    L = Lp * ROW_PACK
    tq = min(ATTN_TQ, L)
    sub = tq // Q_BLOCK
    n_heads = N_GROUPS * HEADS
    slopes = tuple(2.0 ** (-8.0 * (group * HEADS + h + 1) / n_heads) for h in range(HEADS))

    def main(c):
        return pl.BlockSpec((None, None, tq // ROW_PACK, D_MODEL), lambda b, r, n: (b, r, n, c))

    def halo(c):
        return pl.BlockSpec((None, None, Q_BLOCK // ROW_PACK, D_MODEL),
                            lambda b, r, n: (b, r, jnp.maximum(n * sub - 1, 0), c))

    return pl.pallas_call(
        functools.partial(_attn_kernel, dil=dil, slopes=slopes),
        name=f"attn_dil{dil}",
        grid=(B, dil, L // tq),
        in_specs=[main(0), halo(1), main(1), halo(2), main(2)],
        out_specs=[
            pl.BlockSpec((None, None, tq // ROW_PACK, D_MODEL), lambda b, r, n: (b, r, n, 0)),
            pl.BlockSpec((None, None, tq, LSE_LANES), lambda b, r, n: (b, r, n, 0)),
        ],
        out_shape=[
            jax.ShapeDtypeStruct((B, dil, Lp, D_MODEL), jnp.int32),
            jax.ShapeDtypeStruct((B, dil, L, LSE_LANES), F32),
        ],
        scratch_shapes=[pltpu.VMEM((HEADS, Q_BLOCK, 2 * Q_BLOCK), F32)],
        compiler_params=pltpu.CompilerParams(
            dimension_semantics=("arbitrary", "arbitrary", "arbitrary"),
            vmem_limit_bytes=VMEM_LIMIT_BYTES),
    )(qkv, qkv, qkv, qkv, qkv)


def _post_attn_kernel(x_ref, o1_ref, o2_ref, o3_ref, l1_ref, l2_ref, l3_ref, wo_ref,
                      gf_ref, wgu_ref, wd_ref, out_ref, o_scr, l_scr):
    tm = x_ref.shape[0]
    o_refs = (o1_ref, o2_ref, o3_ref)
    l_refs = (l1_ref, l2_ref, l3_ref)
    for g, (_, dil) in enumerate(ATTN_PAIRS):
        rows = tm // dil
        for r in range(dil):
            l_scr[g, pl.ds(r, rows, stride=dil), :] = l_refs[g][r]
            for h in range(HEADS):
                o_scr[g, h, pl.ds(r, rows, stride=dil), :] = (
                    _unpack_rows(o_refs[g][r, :, h * HEAD_DIM:(h + 1) * HEAD_DIM]).astype(F32))
    lse = [l_scr[g] for g in range(N_GROUPS)]
    m = jnp.maximum(jnp.maximum(lse[0], lse[1]), lse[2])
    e = [jnp.exp(l - m) for l in lse]
    inv = 1.0 / (e[0] + e[1] + e[2])
    wts = [ei * inv for ei in e]
    heads = []
    for h in range(HEADS):
        acc = wts[0][:, h:h + 1] * o_scr[0, h]
        for g in range(1, N_GROUPS):
            acc = acc + wts[g][:, h:h + 1] * o_scr[g, h]
        heads.append(acc.astype(BF16))
    merged = jnp.concatenate(heads, axis=-1)
    x1 = x_ref[...] + _dot(merged, wo_ref[...])
    out_ref[...] = _ffn(x1, gf_ref[...], wgu_ref, wd_ref)


def _post_attn(x, outs, lses, w_out, g_ffn, w_gu, w_down):
    B, S, D = x.shape
    tm = POST_TM
    tile = pl.BlockSpec((None, tm, D), lambda b, i: (b, i, 0))
    o_specs = [pl.BlockSpec((None, dil, tm // dil // ROW_PACK, D), lambda b, i: (b, 0, i, 0))
               for _, dil in ATTN_PAIRS]
    l_specs = [pl.BlockSpec((None, dil, tm // dil, LSE_LANES), lambda b, i: (b, 0, i, 0))
               for _, dil in ATTN_PAIRS]
    return pl.pallas_call(
        _post_attn_kernel,
        name="post_attn",
        grid=(B, S // tm),
        in_specs=[tile, *o_specs, *l_specs, _resident((D, D)), _resident((1, D)),
                  _resident((D, 2 * D_FF)), _resident((D_FF, D))],
        out_specs=tile,
        out_shape=jax.ShapeDtypeStruct((B, S, D), F32),
        scratch_shapes=[pltpu.VMEM((N_GROUPS, HEADS, tm, HEAD_DIM), F32),
                        pltpu.VMEM((N_GROUPS, tm, LSE_LANES), F32)],
        compiler_params=pltpu.CompilerParams(
            dimension_semantics=("arbitrary", "arbitrary"), vmem_limit_bytes=VMEM_LIMIT_BYTES),
    )(x, *outs, *lses, w_out, g_ffn, w_gu, w_down)


def _pool_ffn_kernel(x_ref, gp_ref, win_ref, wg_ref, sc_ref, gf_ref, wgu_ref, wd_ref, gn_ref,
                     out_ref, u_scr):
    tm = x_ref.shape[0]
    i = pl.program_id(1)
    x = x_ref[...]
    u = _dot(_rms(x, gp_ref[...]).astype(BF16), win_ref[...])

    @pl.when(i == 0)
    def _():
        u_scr[0:POOL_HALO, :] = jnp.zeros((POOL_HALO, D_MODEL), F32)

    @pl.when(i > 0)
    def _():
        u_scr[0:POOL_HALO, :] = u_scr[tm:tm + POOL_HALO, :]

    u_scr[POOL_HALO:, :] = u
    pos = i * tm + lax.broadcasted_iota(jnp.int32, (tm, 1), 0) + 1
    ys = []
    for g, w in enumerate(POOL_WINDOWS):
        cols = slice(g * POOL_GROUP_DIM, (g + 1) * POOL_GROUP_DIM)
        s = u_scr[:, cols]
        shift = 1
        while shift < w:
            s = s + pltpu.roll(s, shift, axis=0)
            shift *= 2
        count = jnp.minimum(pos, w).astype(F32)
        y = s[POOL_HALO:, :] / count - u[:, cols]
        ys.append(_dot(y.astype(BF16), wg_ref[g]))
    y = jnp.concatenate(ys, axis=-1) * sc_ref[...]
    x2 = x + y
    x3 = _ffn(x2, gf_ref[...], wgu_ref, wd_ref)
    out_ref[...] = _rms(x3, gn_ref[...])


def _pool_ffn(x, g_pool, w_in, w_group, scale, g_ffn, w_gu, w_down, g_final):
    B, S, D = x.shape
    tm = POOL_TM
    tile = pl.BlockSpec((None, tm, D), lambda b, i: (b, i, 0))
    return pl.pallas_call(
        _pool_ffn_kernel,
        name="pool_ffn",
        grid=(B, S // tm),
        in_specs=[tile, _resident((1, D)), _resident((D, D)),
                  _resident((len(POOL_WINDOWS), POOL_GROUP_DIM, POOL_GROUP_DIM)),
                  _resident((1, D)), _resident((1, D)),
                  _resident((D, 2 * D_FF)), _resident((D_FF, D)), _resident((1, D))],
        out_specs=tile,
        out_shape=jax.ShapeDtypeStruct((B, S, D), F32),
        scratch_shapes=[pltpu.VMEM((POOL_HALO + tm, D), F32)],
        compiler_params=pltpu.CompilerParams(
            dimension_semantics=("arbitrary", "arbitrary"), vmem_limit_bytes=VMEM_LIMIT_BYTES),
    )(x, g_pool, w_in, w_group, scale, g_ffn, w_gu, w_down, g_final)


def kernel(x, attn_norm, w_qkv, w_attn_out, pool_norm, w_pool_in, w_pool_group, pool_scale,
           ffn_norm, w_ffn_gate_up, w_ffn_down, final_norm):
    assert attn_norm.shape[0] == 1 and pool_norm.shape[0] == 1 and ffn_norm.shape[0] == 2
    bf = lambda w: w.astype(BF16)
    row = lambda v: v.reshape(1, D_MODEL)

    qkvs = _qkv_proj(x, row(attn_norm[0]), bf(w_qkv[0]))
    outs, lses = zip(*[_attention(qkvs[g], g) for g in range(N_GROUPS)])
    x1 = _post_attn(x, outs, lses, bf(w_attn_out[0]), row(ffn_norm[0]),
                    bf(w_ffn_gate_up[0]), bf(w_ffn_down[0]))
    return _pool_ffn(x1, row(pool_norm[0]), bf(w_pool_in[0]), bf(w_pool_group[0]), row(pool_scale[0]),
                     row(ffn_norm[1]), bf(w_ffn_gate_up[1]), bf(w_ffn_down[1]), row(final_norm))
```

```python
import functools

import jax
import jax.numpy as jnp
from jax import lax
from jax.experimental import pallas as pl
from jax.experimental.pallas import tpu as pltpu

F32 = jnp.float32
BF16 = jnp.bfloat16

D_MODEL = 1024
HEADS = 8
HEAD_DIM = 128
ATTN_PAIRS = ((128, 1), (512, 4), (2048, 16))
N_GROUPS = len(ATTN_PAIRS)
GROUP_QKV = 3 * D_MODEL
Q_BLOCK = 128
POOL_WINDOWS = (2, 4, 8, 16)
POOL_GROUP_DIM = D_MODEL // len(POOL_WINDOWS)
POOL_HALO = 16
D_FF = 2816
RMS_EPS = 1e-6
LOG2_E = 1.4426950408889634
LN_2 = 0.6931471805599453
Q_SCALE = LOG2_E * HEAD_DIM ** -0.5
LANES = 128
ROW_PACK = 2
N_CHUNKS = D_MODEL // LANES
LSE_LANES = LANES

VMEM_LIMIT_BYTES = 56 * 1024 * 1024

QKV_TM = 512
ATTN_ROWS = 512
POST_TM = 512
POOL_TM = 512


def _rms(x, gamma):
    return x * lax.rsqrt(jnp.mean(x * x, axis=-1, keepdims=True) + RMS_EPS) * gamma


def _dot(a, b):
    return jnp.dot(a, b, preferred_element_type=F32)


def _ffn_from_h(x, h, wgu_ref, wd_ref):
    gu = _dot(h, wgu_ref[...])
    gate, up = gu[:, :D_FF], gu[:, D_FF:]
    act = (gate * jax.nn.sigmoid(gate) * up).astype(BF16)
    return x + _dot(act, wd_ref[...])


def _pack_rows(x):
    return pltpu.bitcast(x, jnp.int32)


def _unpack_rows(x):
    return pltpu.bitcast(x, BF16)


def _resident(shape):
    return pl.BlockSpec(shape, lambda *_: (0,) * len(shape), pipeline_mode=pl.Buffered(1))


def _qkv_kernel(x_ref, g_ref, w_ref, o1_ref, o2_ref, o3_ref, h_scr, hp_scr):
    tm = x_ref.shape[0]
    h = _rms(x_ref[...], g_ref[...])
    for c in range(N_CHUNKS):
        h_scr[c] = h[:, c * LANES:(c + 1) * LANES]
    out_refs = (o1_ref, o2_ref, o3_ref)
    for g, (_, dil) in enumerate(ATTN_PAIRS):
        rows = tm // dil
        if dil == 1:
            hp = h.astype(BF16)
        else:
            for r in range(dil):
                for c in range(N_CHUNKS):
                    hp_scr[r * rows:(r + 1) * rows, c * LANES:(c + 1) * LANES] = (
                        h_scr[c, pl.ds(r, rows, stride=dil), :].astype(BF16))
            hp = hp_scr[...]
        y = _dot(hp, w_ref[:, g * GROUP_QKV:(g + 1) * GROUP_QKV])
        yq = (y[:, :D_MODEL] * Q_SCALE).astype(BF16)
        ykv = y[:, D_MODEL:].astype(BF16)
        for r in range(dil):
            out_refs[g][r, :, :D_MODEL] = _pack_rows(yq[r * rows:(r + 1) * rows, :])
            out_refs[g][r, :, D_MODEL:] = _pack_rows(ykv[r * rows:(r + 1) * rows, :])


def _qkv_proj(x, gamma, w_qkv):
    B, S, D = x.shape
    tm = QKV_TM
    out_shape = [jax.ShapeDtypeStruct((B, dil, S // dil // ROW_PACK, GROUP_QKV), jnp.int32)
                 for _, dil in ATTN_PAIRS]
    out_specs = [pl.BlockSpec((None, dil, tm // dil // ROW_PACK, GROUP_QKV), lambda b, i: (b, 0, i, 0))
                 for _, dil in ATTN_PAIRS]
    return pl.pallas_call(
        _qkv_kernel,
        name="qkv_proj",
        grid=(B, S // tm),
        in_specs=[
            pl.BlockSpec((None, tm, D), lambda b, i: (b, i, 0)),
            _resident((1, D)),
            _resident((D, N_GROUPS * GROUP_QKV)),
        ],
        out_specs=out_specs,
        out_shape=out_shape,
        scratch_shapes=[pltpu.VMEM((N_CHUNKS, tm, LANES), F32), pltpu.VMEM((tm, D), BF16)],
        compiler_params=pltpu.CompilerParams(
            dimension_semantics=("arbitrary", "arbitrary"), vmem_limit_bytes=VMEM_LIMIT_BYTES),
    )(x, gamma, w_qkv)


def _attn_kernel(*refs, dil, slopes, has_halo):
    if has_halo:
        q_ref, kh_ref, km_ref, vh_ref, vm_ref, o_ref, lse_ref, bias_scr = refs
    else:
        q_ref, km_ref, vm_ref, o_ref, lse_ref, bias_scr = refs
    n_res, tq, _ = lse_ref.shape
    n = pl.program_id(2)
    ki = lax.broadcasted_iota(jnp.int32, (Q_BLOCK, 2 * Q_BLOCK), 1)

    @pl.when((pl.program_id(0) == 0) & (pl.program_id(1) == 0) & (n == 0))
    def _():
        qi = lax.broadcasted_iota(jnp.int32, (Q_BLOCK, 2 * Q_BLOCK), 0)
        delta = Q_BLOCK + qi - ki
        valid = (delta >= 0) & (delta <= Q_BLOCK)
        dist = (delta * dil).astype(F32)
        for h in range(HEADS):
            bias_scr[h] = jnp.where(valid, (-slopes[h] * LOG2_E) * dist, -jnp.inf)

    first_lo = jnp.where(n == 0, Q_BLOCK, 0)
    lane = lax.broadcasted_iota(jnp.int32, (Q_BLOCK, LSE_LANES), 1)
    ones = jnp.ones((2 * Q_BLOCK, HEAD_DIM), BF16)
    pb = Q_BLOCK // ROW_PACK
    for r in range(n_res):
        for i in range(tq // Q_BLOCK):
            rows = slice(i * Q_BLOCK, (i + 1) * Q_BLOCK)
            prows = slice(i * pb, (i + 1) * pb)
            m_tile = jnp.zeros((Q_BLOCK, LSE_LANES), F32)
            den_tile = jnp.ones((Q_BLOCK, LSE_LANES), F32)
            for h in range(HEADS):
                cols = slice(h * HEAD_DIM, (h + 1) * HEAD_DIM)
                q = _unpack_rows(q_ref[r, prows, cols])
                if i > 0:
                    k = km_ref[r, (i - 1) * pb:(i + 1) * pb, cols]
                    v = vm_ref[r, (i - 1) * pb:(i + 1) * pb, cols]
                elif has_halo:
                    k = jnp.concatenate([kh_ref[r, :, cols], km_ref[r, 0:pb, cols]], axis=0)
                    v = jnp.concatenate([vh_ref[r, :, cols], vm_ref[r, 0:pb, cols]], axis=0)
                else:
                    k = jnp.concatenate([km_ref[r, 0:pb, cols]] * 2, axis=0)
                    v = jnp.concatenate([vm_ref[r, 0:pb, cols]] * 2, axis=0)
                k, v = _unpack_rows(k), _unpack_rows(v)
                t = lax.dot_general(q, k, (((1,), (1,)), ((), ())), preferred_element_type=F32)
                t = t + bias_scr[h]
                if i == 0:
                    t = jnp.where(ki >= first_lo, t, -jnp.inf)
                m = jnp.max(t, axis=-1, keepdims=True)
                p = jnp.exp2(t - m).astype(BF16)
                ov = _dot(p, jnp.concatenate([v, ones], axis=1))
                den = ov[:, HEAD_DIM:]
                o_ref[r, prows, cols] = _pack_rows((ov[:, :HEAD_DIM] / den).astype(BF16))
                m_tile = jnp.where(lane == h, m, m_tile)
                den_tile = jnp.where(lane == h, den, den_tile)
            lse_ref[r, rows, :] = (m_tile + jnp.log2(den_tile)) * LN_2


def _attention(qkv, group):
    _, dil = ATTN_PAIRS[group]
    B, _, Lp, _ = qkv.shape
    L = Lp * ROW_PACK
    tq = min(ATTN_ROWS, L)
    n_res = ATTN_ROWS // tq
    sub = tq // Q_BLOCK
    has_halo = L > tq
    n_heads = N_GROUPS * HEADS
    slopes = tuple(2.0 ** (-8.0 * (group * HEADS + h + 1) / n_heads) for h in range(HEADS))

    def main(c):
        return pl.BlockSpec((None, n_res, tq // ROW_PACK, D_MODEL), lambda b, r, n: (b, r, n, c))

    def halo(c):
        return pl.BlockSpec((None, n_res, Q_BLOCK // ROW_PACK, D_MODEL),
                            lambda b, r, n: (b, r, jnp.maximum(n * sub - 1, 0), c))

    in_specs = [main(0), halo(1), main(1), halo(2), main(2)] if has_halo else [main(0), main(1), main(2)]
    return pl.pallas_call(
        functools.partial(_attn_kernel, dil=dil, slopes=slopes, has_halo=has_halo),
        name=f"attn_dil{dil}",
        grid=(B, dil // n_res, L // tq),
        in_specs=in_specs,
        out_specs=[
            pl.BlockSpec((None, n_res, tq // ROW_PACK, D_MODEL), lambda b, r, n: (b, r, n, 0)),
            pl.BlockSpec((None, n_res, tq, LSE_LANES), lambda b, r, n: (b, r, n, 0)),
        ],
        out_shape=[
            jax.ShapeDtypeStruct((B, dil, Lp, D_MODEL), jnp.int32),
            jax.ShapeDtypeStruct((B, dil, L, LSE_LANES), F32),
        ],
        scratch_shapes=[pltpu.VMEM((HEADS, Q_BLOCK, 2 * Q_BLOCK), F32)],
        compiler_params=pltpu.CompilerParams(
            dimension_semantics=("arbitrary", "arbitrary", "arbitrary"),
            vmem_limit_bytes=VMEM_LIMIT_BYTES),
    )(*([qkv] * len(in_specs)))


def _post_attn_kernel(x_ref, o1_ref, o2_ref, o3_ref, l1_ref, l2_ref, l3_ref, wo_ref,
                      gf_ref, wgu_ref, wd_ref, out_ref, o_scr, l_scr, x1_scr, h_scr):
    tm = x_ref.shape[0]

    @pl.when(pl.program_id(0) == 0)
    def _():
        x1_scr[...] = jnp.zeros_like(x1_scr)
        h_scr[...] = jnp.zeros_like(h_scr)

    out_ref[...] = _ffn_from_h(x1_scr[...], h_scr[...], wgu_ref, wd_ref)

    o_refs = (o1_ref, o2_ref, o3_ref)
    l_refs = (l1_ref, l2_ref, l3_ref)
    for g, (_, dil) in enumerate(ATTN_PAIRS):
        rows = tm // dil
        for r in range(dil):
            l_scr[g, pl.ds(r, rows, stride=dil), :] = l_refs[g][r]
            for h in range(HEADS):
                o_scr[g, h, pl.ds(r, rows, stride=dil), :] = (
                    _unpack_rows(o_refs[g][r, :, h * HEAD_DIM:(h + 1) * HEAD_DIM]).astype(F32))
    lse = [l_scr[g] for g in range(N_GROUPS)]
    m = jnp.maximum(jnp.maximum(lse[0], lse[1]), lse[2])
    e = [jnp.exp(l - m) for l in lse]
    inv = 1.0 / (e[0] + e[1] + e[2])
    wts = [ei * inv for ei in e]
    heads = []
    for h in range(HEADS):
        acc = wts[0][:, h:h + 1] * o_scr[0, h]
        for g in range(1, N_GROUPS):
            acc = acc + wts[g][:, h:h + 1] * o_scr[g, h]
        heads.append(acc.astype(BF16))
    merged = jnp.concatenate(heads, axis=-1)
    x1 = x_ref[...] + _dot(merged, wo_ref[...])
    x1_scr[...] = x1
    h_scr[...] = _rms(x1, gf_ref[...]).astype(BF16)


def _skewed_tiles(n_tiles, tiles_per_seq):
    def in_map(s):
        t = jnp.minimum(s, n_tiles - 1)
        return t // tiles_per_seq, t % tiles_per_seq

    def out_map(s):
        t = jnp.maximum(s - 1, 0)
        return t // tiles_per_seq, t % tiles_per_seq

    return in_map, out_map


def _post_attn(x, outs, lses, w_out, g_ffn, w_gu, w_down):
    B, S, D = x.shape
    tm = POST_TM
    n_tiles = B * S // tm
    in_map, out_map = _skewed_tiles(n_tiles, S // tm)

    def folded(s):
        b, i = in_map(s)
        return b, 0, i, 0

    o_specs = [pl.BlockSpec((None, dil, tm // dil // ROW_PACK, D), folded) for _, dil in ATTN_PAIRS]
    l_specs = [pl.BlockSpec((None, dil, tm // dil, LSE_LANES), folded) for _, dil in ATTN_PAIRS]
    return pl.pallas_call(
        _post_attn_kernel,
        name="post_attn",
        grid=(n_tiles + 1,),
        in_specs=[pl.BlockSpec((None, tm, D), lambda s: (*in_map(s), 0)),
                  *o_specs, *l_specs, _resident((D, D)), _resident((1, D)),
                  _resident((D, 2 * D_FF)), _resident((D_FF, D))],
        out_specs=pl.BlockSpec((None, tm, D), lambda s: (*out_map(s), 0)),
        out_shape=jax.ShapeDtypeStruct((B, S, D), F32),
        scratch_shapes=[pltpu.VMEM((N_GROUPS, HEADS, tm, HEAD_DIM), F32),
                        pltpu.VMEM((N_GROUPS, tm, LSE_LANES), F32),
                        pltpu.VMEM((tm, D), F32), pltpu.VMEM((tm, D), BF16)],
        compiler_params=pltpu.CompilerParams(
            dimension_semantics=("arbitrary",), vmem_limit_bytes=VMEM_LIMIT_BYTES),
    )(x, *outs, *lses, w_out, g_ffn, w_gu, w_down)


def _pool_ffn_kernel(x_ref, gp_ref, win_ref, wg_ref, sc_ref, gf_ref, wgu_ref, wd_ref, gn_ref,
                     out_ref, u_scr, x2_scr, h_scr, *, n_tiles, tiles_per_seq):
    tm = x_ref.shape[0]
    s = pl.program_id(0)
    i = jnp.minimum(s, n_tiles - 1) % tiles_per_seq

    @pl.when(s == 0)
    def _():
        u_scr[...] = jnp.zeros_like(u_scr)
        x2_scr[...] = jnp.zeros_like(x2_scr)
        h_scr[...] = jnp.zeros_like(h_scr)

    x3 = _ffn_from_h(x2_scr[...], h_scr[...], wgu_ref, wd_ref)
    out_ref[...] = _rms(x3, gn_ref[...])

    x = x_ref[...]
    u = _dot(_rms(x, gp_ref[...]).astype(BF16), win_ref[...])
    prev_tail = u_scr[tm:tm + POOL_HALO, :]
    u_scr[0:POOL_HALO, :] = jnp.where(i == 0, 0.0, prev_tail)
    u_scr[POOL_HALO:, :] = u
    pos = i * tm + lax.broadcasted_iota(jnp.int32, (tm, 1), 0) + 1
    ys = []
    for g, w in enumerate(POOL_WINDOWS):
        cols = slice(g * POOL_GROUP_DIM, (g + 1) * POOL_GROUP_DIM)
        wsum = u_scr[:, cols]
        shift = 1
        while shift < w:
            wsum = wsum + pltpu.roll(wsum, shift, axis=0)
            shift *= 2
        count = jnp.minimum(pos, w).astype(F32)
        y = wsum[POOL_HALO:, :] / count - u[:, cols]
        ys.append(_dot(y.astype(BF16), wg_ref[g]))
    y = jnp.concatenate(ys, axis=-1) * sc_ref[...]
    x2 = x + y
    x2_scr[...] = x2
    h_scr[...] = _rms(x2, gf_ref[...]).astype(BF16)


def _pool_ffn(x, g_pool, w_in, w_group, scale, g_ffn, w_gu, w_down, g_final):
    B, S, D = x.shape
    tm = POOL_TM
    n_tiles = B * S // tm
    in_map, out_map = _skewed_tiles(n_tiles, S // tm)
    return pl.pallas_call(
        functools.partial(_pool_ffn_kernel, n_tiles=n_tiles, tiles_per_seq=S // tm),
        name="pool_ffn",
        grid=(n_tiles + 1,),
        in_specs=[pl.BlockSpec((None, tm, D), lambda s: (*in_map(s), 0)),
                  _resident((1, D)), _resident((D, D)),
                  _resident((len(POOL_WINDOWS), POOL_GROUP_DIM, POOL_GROUP_DIM)),
                  _resident((1, D)), _resident((1, D)),
                  _resident((D, 2 * D_FF)), _resident((D_FF, D)), _resident((1, D))],
        out_specs=pl.BlockSpec((None, tm, D), lambda s: (*out_map(s), 0)),
        out_shape=jax.ShapeDtypeStruct((B, S, D), F32),
        scratch_shapes=[pltpu.VMEM((POOL_HALO + tm, D), F32),
                        pltpu.VMEM((tm, D), F32), pltpu.VMEM((tm, D), BF16)],
        compiler_params=pltpu.CompilerParams(
            dimension_semantics=("arbitrary",), vmem_limit_bytes=VMEM_LIMIT_BYTES),
    )(x, g_pool, w_in, w_group, scale, g_ffn, w_gu, w_down, g_final)


def kernel(x, attn_norm, w_qkv, w_attn_out, pool_norm, w_pool_in, w_pool_group, pool_scale,
           ffn_norm, w_ffn_gate_up, w_ffn_down, final_norm):
    assert attn_norm.shape[0] == 1 and pool_norm.shape[0] == 1 and ffn_norm.shape[0] == 2
    bf = lambda w: w.astype(BF16)
    row = lambda v: v.reshape(1, D_MODEL)

    qkvs = _qkv_proj(x, row(attn_norm[0]), bf(w_qkv[0]))
    outs, lses = zip(*[_attention(qkvs[g], g) for g in range(N_GROUPS)])
    x1 = _post_attn(x, outs, lses, bf(w_attn_out[0]), row(ffn_norm[0]),
                    bf(w_ffn_gate_up[0]), bf(w_ffn_down[0]))
    return _pool_ffn(x1, row(pool_norm[0]), bf(w_pool_in[0]), bf(w_pool_group[0]), row(pool_scale[0]),
                     row(ffn_norm[1]), bf(w_ffn_gate_up[1]), bf(w_ffn_down[1]), row(final_norm))
```

```python
import functools

import jax
import jax.numpy as jnp
from jax import lax
from jax.experimental import pallas as pl
from jax.experimental.pallas import tpu as pltpu

F32 = jnp.float32
BF16 = jnp.bfloat16

D_MODEL = 1024
HEADS = 8
HEAD_DIM = 128
ATTN_PAIRS = ((128, 1), (512, 4), (2048, 16))
N_GROUPS = len(ATTN_PAIRS)
GROUP_QKV = 3 * D_MODEL
Q_BLOCK = 128
POOL_WINDOWS = (2, 4, 8, 16)
POOL_GROUP_DIM = D_MODEL // len(POOL_WINDOWS)
POOL_HALO = 16
D_FF = 2816
RMS_EPS = 1e-6
LOG2_E = 1.4426950408889634
LN_2 = 0.6931471805599453
Q_SCALE = LOG2_E * HEAD_DIM ** -0.5
LANES = 128
ROW_PACK = 2
N_CHUNKS = D_MODEL // LANES
LSE_LANES = LANES

VMEM_LIMIT_BYTES = 56 * 1024 * 1024

QKV_TM = 512
ATTN_ROWS = 512
POST_TM = 512
POOL_TM = 512


def _rms(x, gamma):
    return x * lax.rsqrt(jnp.mean(x * x, axis=-1, keepdims=True) + RMS_EPS) * gamma


def _dot(a, b):
    return jnp.dot(a, b, preferred_element_type=F32)


def _ffn_from_h(x, h, wgu_ref, wd_ref):
    gu = _dot(h, wgu_ref[...])
    gate, up = gu[:, :D_FF], gu[:, D_FF:]
    act = (gate * jax.nn.sigmoid(gate) * up).astype(BF16)
    return x + _dot(act, wd_ref[...])


def _pack_rows(x):
    return pltpu.bitcast(x, jnp.int32)


def _unpack_rows(x):
    return pltpu.bitcast(x, BF16)


def _resident(shape):
    return pl.BlockSpec(shape, lambda *_: (0,) * len(shape), pipeline_mode=pl.Buffered(1))


def _build_bias(bias_ref, dil, slopes):
    qi = lax.broadcasted_iota(jnp.int32, (Q_BLOCK, 2 * Q_BLOCK), 0)
    ki = lax.broadcasted_iota(jnp.int32, (Q_BLOCK, 2 * Q_BLOCK), 1)
    delta = Q_BLOCK + qi - ki
    valid = (delta >= 0) & (delta <= Q_BLOCK)
    dist = (delta * dil).astype(F32)
    for h in range(HEADS):
        bias_ref[h] = jnp.where(valid, (-slopes[h] * LOG2_E) * dist, -jnp.inf)


def _alibi_slopes(group):
    n_heads = N_GROUPS * HEADS
    return tuple(2.0 ** (-8.0 * (group * HEADS + h + 1) / n_heads) for h in range(HEADS))


def _head_cols(h, offset=0):
    return slice(offset + h * HEAD_DIM, offset + (h + 1) * HEAD_DIM)


def _attend_block(load_q, load_kv, bias_ref, first_lo, store_o, store_lse):
    ki = lax.broadcasted_iota(jnp.int32, (Q_BLOCK, 2 * Q_BLOCK), 1)
    lane = lax.broadcasted_iota(jnp.int32, (Q_BLOCK, LSE_LANES), 1)
    ones = jnp.ones((2 * Q_BLOCK, HEAD_DIM), BF16)
    m_tile = jnp.zeros((Q_BLOCK, LSE_LANES), F32)
    den_tile = jnp.ones((Q_BLOCK, LSE_LANES), F32)
    for h in range(HEADS):
        q = load_q(h)
        k, v = load_kv(h)
        t = lax.dot_general(q, k, (((1,), (1,)), ((), ())), preferred_element_type=F32)
        t = t + bias_ref[h]
        if first_lo is not None:
            t = jnp.where(ki >= first_lo, t, -jnp.inf)
        m = jnp.max(t, axis=-1, keepdims=True)
        p = jnp.exp2(t - m).astype(BF16)
        ov = _dot(p, jnp.concatenate([v, ones], axis=1))
        den = ov[:, HEAD_DIM:]
        store_o(h, (ov[:, :HEAD_DIM] / den).astype(BF16))
        m_tile = jnp.where(lane == h, m, m_tile)
        den_tile = jnp.where(lane == h, den, den_tile)
    store_lse((m_tile + jnp.log2(den_tile)) * LN_2)


FUSED_GROUPS = 2


def _qkv_attn_kernel(x_ref, g_ref, w_ref, o1_ref, l1_ref, o2_ref, l2_ref, qkv3_ref,
                     h_scr, hp2_scr, hp3_scr, q1_scr, q2_scr, kv1_scr, kv2_scr, bias_scr, *, slopes):
    tm = x_ref.shape[0]
    i = pl.program_id(1)

    @pl.when((pl.program_id(0) == 0) & (i == 0))
    def _():
        for g in range(FUSED_GROUPS):
            _build_bias(bias_scr.at[g], ATTN_PAIRS[g][1], slopes[g])
        kv1_scr[0:Q_BLOCK, :] = jnp.zeros((Q_BLOCK, 2 * D_MODEL), BF16)
        kv2_scr[:, 0:Q_BLOCK, :] = jnp.zeros((kv2_scr.shape[0], Q_BLOCK, 2 * D_MODEL), BF16)

    h = _rms(x_ref[...], g_ref[...])
    for c in range(N_CHUNKS):
        h_scr[c] = h[:, c * LANES:(c + 1) * LANES]

    def fold(hp_ref, dil):
        rows = tm // dil
        for r in range(dil):
            for c in range(N_CHUNKS):
                hp_ref[r * rows:(r + 1) * rows, c * LANES:(c + 1) * LANES] = (
                    h_scr[c, pl.ds(r, rows, stride=dil), :].astype(BF16))

    def project(hp, g):
        y = _dot(hp, w_ref[:, g * GROUP_QKV:(g + 1) * GROUP_QKV])
        return (y[:, :D_MODEL] * Q_SCALE).astype(BF16), y[:, D_MODEL:].astype(BF16)

    first_lo = jnp.where(i == 0, Q_BLOCK, 0)
    pb = Q_BLOCK // ROW_PACK

    q, kv = project(h.astype(BF16), 0)
    q1_scr[...] = q
    kv1_scr[Q_BLOCK:, :] = kv
    for j in range(tm // Q_BLOCK):
        def store_o(hd, o, j=j):
            o1_ref[0, j * pb:(j + 1) * pb, _head_cols(hd)] = _pack_rows(o)

        def store_lse(lse, j=j):
            l1_ref[0, j * Q_BLOCK:(j + 1) * Q_BLOCK, :] = lse

        keys = slice(j * Q_BLOCK, (j + 2) * Q_BLOCK)
        _attend_block(
            lambda hd, j=j: q1_scr[j * Q_BLOCK:(j + 1) * Q_BLOCK, _head_cols(hd)],
            lambda hd, keys=keys: (kv1_scr[keys, _head_cols(hd)], kv1_scr[keys, _head_cols(hd, D_MODEL)]),
            bias_scr.at[0], first_lo if j == 0 else None, store_o, store_lse)
    kv1_scr[0:Q_BLOCK, :] = kv1_scr[tm:tm + Q_BLOCK, :]

    dil = ATTN_PAIRS[1][1]
    assert tm // dil == Q_BLOCK
    fold(hp2_scr, dil)
    q, kv = project(hp2_scr[...], 1)
    q2_scr[...] = q
    for r in range(dil):
        kv2_scr[r, Q_BLOCK:, :] = kv[r * Q_BLOCK:(r + 1) * Q_BLOCK, :]
    for r in range(dil):
        def store_o(hd, o, r=r):
            o2_ref[r, :, _head_cols(hd)] = _pack_rows(o)

        def store_lse(lse, r=r):
            l2_ref[r] = lse

        _attend_block(
            lambda hd, r=r: q2_scr[r * Q_BLOCK:(r + 1) * Q_BLOCK, _head_cols(hd)],
            lambda hd, r=r: (kv2_scr[r, :, _head_cols(hd)], kv2_scr[r, :, _head_cols(hd, D_MODEL)]),
            bias_scr.at[1], first_lo, store_o, store_lse)
    for r in range(dil):
        kv2_scr[r, 0:Q_BLOCK, :] = kv2_scr[r, Q_BLOCK:, :]

    dil = ATTN_PAIRS[2][1]
    rows = tm // dil
    fold(hp3_scr, dil)
    q, kv = project(hp3_scr[...], 2)
    for r in range(dil):
        qkv3_ref[r, :, :D_MODEL] = _pack_rows(q[r * rows:(r + 1) * rows, :])
        qkv3_ref[r, :, D_MODEL:] = _pack_rows(kv[r * rows:(r + 1) * rows, :])


def _qkv_attn(x, gamma, w_qkv):
    B, S, D = x.shape
    tm = QKV_TM
    dils = [dil for _, dil in ATTN_PAIRS]

    def folded(width, rows_per_word):
        def spec(dil):
            return pl.BlockSpec((None, dil, tm // dil // rows_per_word, width), lambda b, i: (b, 0, i, 0))
        return spec

    o_spec, l_spec = folded(D, ROW_PACK), folded(LSE_LANES, 1)
    o_shape = lambda dil: jax.ShapeDtypeStruct((B, dil, S // dil // ROW_PACK, D), jnp.int32)
    l_shape = lambda dil: jax.ShapeDtypeStruct((B, dil, S // dil, LSE_LANES), F32)
    d2 = dils[2]
    o1, l1, o2, l2, qkv3 = pl.pallas_call(
        functools.partial(_qkv_attn_kernel, slopes=[_alibi_slopes(g) for g in range(FUSED_GROUPS)]),
        name="qkv_attn",
        grid=(B, S // tm),
        in_specs=[
            pl.BlockSpec((None, tm, D), lambda b, i: (b, i, 0)),
            _resident((1, D)),
            _resident((D, N_GROUPS * GROUP_QKV)),
        ],
        out_specs=[o_spec(dils[0]), l_spec(dils[0]), o_spec(dils[1]), l_spec(dils[1]),
                   folded(GROUP_QKV, ROW_PACK)(d2)],
        out_shape=[o_shape(dils[0]), l_shape(dils[0]), o_shape(dils[1]), l_shape(dils[1]),
                   jax.ShapeDtypeStruct((B, d2, S // d2 // ROW_PACK, GROUP_QKV), jnp.int32)],
        scratch_shapes=[
            pltpu.VMEM((N_CHUNKS, tm, LANES), F32),
            pltpu.VMEM((tm, D), BF16), pltpu.VMEM((tm, D), BF16),
            pltpu.VMEM((tm, D), BF16), pltpu.VMEM((tm, D), BF16),
            pltpu.VMEM((Q_BLOCK + tm, 2 * D), BF16),
            pltpu.VMEM((dils[1], 2 * Q_BLOCK, 2 * D), BF16),
            pltpu.VMEM((FUSED_GROUPS, HEADS, Q_BLOCK, 2 * Q_BLOCK), F32),
        ],
        compiler_params=pltpu.CompilerParams(
            dimension_semantics=("arbitrary", "arbitrary"), vmem_limit_bytes=VMEM_LIMIT_BYTES),
    )(x, gamma, w_qkv)
    return (o1, o2), (l1, l2), qkv3


def _attn_kernel(q_ref, k_ref, v_ref, o_ref, lse_ref, bias_scr, *, dil, slopes):
    n_res, tq, _ = lse_ref.shape

    @pl.when((pl.program_id(0) == 0) & (pl.program_id(1) == 0))
    def _():
        _build_bias(bias_scr, dil, slopes)

    pb = Q_BLOCK // ROW_PACK
    for r in range(n_res):
        for i in range(tq // Q_BLOCK):
            prows = slice(i * pb, (i + 1) * pb)
            krows = slice((i - 1) * pb, (i + 1) * pb) if i > 0 else None

            def load_kv(hd, r=r, krows=krows):
                if krows is None:
                    k = jnp.concatenate([k_ref[r, 0:pb, _head_cols(hd)]] * 2, axis=0)
                    v = jnp.concatenate([v_ref[r, 0:pb, _head_cols(hd)]] * 2, axis=0)
                else:
                    k, v = k_ref[r, krows, _head_cols(hd)], v_ref[r, krows, _head_cols(hd)]
                return _unpack_rows(k), _unpack_rows(v)

            def store_o(hd, o, r=r, prows=prows):
                o_ref[r, prows, _head_cols(hd)] = _pack_rows(o)

            def store_lse(lse, r=r, i=i):
                lse_ref[r, i * Q_BLOCK:(i + 1) * Q_BLOCK, :] = lse

            _attend_block(
                lambda hd, r=r, prows=prows: _unpack_rows(q_ref[r, prows, _head_cols(hd)]),
                load_kv, bias_scr, Q_BLOCK if i == 0 else None, store_o, store_lse)


def _attention(qkv, group):
    _, dil = ATTN_PAIRS[group]
    B, _, Lp, _ = qkv.shape
    L = Lp * ROW_PACK
    assert L <= ATTN_ROWS
    n_res = ATTN_ROWS // L

    def part(c):
        return pl.BlockSpec((None, n_res, Lp, D_MODEL), lambda b, r: (b, r, 0, c))

    return pl.pallas_call(
        functools.partial(_attn_kernel, dil=dil, slopes=_alibi_slopes(group)),
        name=f"attn_dil{dil}",
        grid=(B, dil // n_res),
        in_specs=[part(0), part(1), part(2)],
        out_specs=[
            pl.BlockSpec((None, n_res, Lp, D_MODEL), lambda b, r: (b, r, 0, 0)),
            pl.BlockSpec((None, n_res, L, LSE_LANES), lambda b, r: (b, r, 0, 0)),
        ],
        out_shape=[
            jax.ShapeDtypeStruct((B, dil, Lp, D_MODEL), jnp.int32),
            jax.ShapeDtypeStruct((B, dil, L, LSE_LANES), F32),
        ],
        scratch_shapes=[pltpu.VMEM((HEADS, Q_BLOCK, 2 * Q_BLOCK), F32)],
        compiler_params=pltpu.CompilerParams(
            dimension_semantics=("arbitrary", "arbitrary"), vmem_limit_bytes=VMEM_LIMIT_BYTES),
    )(qkv, qkv, qkv)


def _post_attn_kernel(x_ref, o1_ref, o2_ref, o3_ref, l1_ref, l2_ref, l3_ref, wo_ref,
                      gf_ref, wgu_ref, wd_ref, out_ref, o_scr, l_scr, x1_scr, h_scr):
    tm = x_ref.shape[0]

    @pl.when(pl.program_id(0) == 0)
    def _():
        x1_scr[...] = jnp.zeros_like(x1_scr)
        h_scr[...] = jnp.zeros_like(h_scr)

    out_ref[...] = _ffn_from_h(x1_scr[...], h_scr[...], wgu_ref, wd_ref)

    o_refs = (o1_ref, o2_ref, o3_ref)
    l_refs = (l1_ref, l2_ref, l3_ref)
    for g, (_, dil) in enumerate(ATTN_PAIRS):
        rows = tm // dil
        for r in range(dil):
            l_scr[g, pl.ds(r, rows, stride=dil), :] = l_refs[g][r]
            for h in range(HEADS):
                o_scr[g, h, pl.ds(r, rows, stride=dil), :] = (
                    _unpack_rows(o_refs[g][r, :, h * HEAD_DIM:(h + 1) * HEAD_DIM]).astype(F32))
    lse = [l_scr[g] for g in range(N_GROUPS)]
    m = jnp.maximum(jnp.maximum(lse[0], lse[1]), lse[2])
    e = [jnp.exp(l - m) for l in lse]
    inv = 1.0 / (e[0] + e[1] + e[2])
    wts = [ei * inv for ei in e]
    heads = []
    for h in range(HEADS):
        acc = wts[0][:, h:h + 1] * o_scr[0, h]
        for g in range(1, N_GROUPS):
            acc = acc + wts[g][:, h:h + 1] * o_scr[g, h]
        heads.append(acc.astype(BF16))
    merged = jnp.concatenate(heads, axis=-1)
    x1 = x_ref[...] + _dot(merged, wo_ref[...])
    x1_scr[...] = x1
    h_scr[...] = _rms(x1, gf_ref[...]).astype(BF16)


def _skewed_tiles(n_tiles, tiles_per_seq):
    def in_map(s):
        t = jnp.minimum(s, n_tiles - 1)
        return t // tiles_per_seq, t % tiles_per_seq

    def out_map(s):
        t = jnp.maximum(s - 1, 0)
        return t // tiles_per_seq, t % tiles_per_seq

    return in_map, out_map


def _post_attn(x, outs, lses, w_out, g_ffn, w_gu, w_down):
    B, S, D = x.shape
    tm = POST_TM
    n_tiles = B * S // tm
    in_map, out_map = _skewed_tiles(n_tiles, S // tm)

    def folded(s):
        b, i = in_map(s)
        return b, 0, i, 0

    o_specs = [pl.BlockSpec((None, dil, tm // dil // ROW_PACK, D), folded) for _, dil in ATTN_PAIRS]
    l_specs = [pl.BlockSpec((None, dil, tm // dil, LSE_LANES), folded) for _, dil in ATTN_PAIRS]
    return pl.pallas_call(
        _post_attn_kernel,
        name="post_attn",
        grid=(n_tiles + 1,),
        in_specs=[pl.BlockSpec((None, tm, D), lambda s: (*in_map(s), 0)),
                  *o_specs, *l_specs, _resident((D, D)), _resident((1, D)),
                  _resident((D, 2 * D_FF)), _resident((D_FF, D))],
        out_specs=pl.BlockSpec((None, tm, D), lambda s: (*out_map(s), 0)),
        out_shape=jax.ShapeDtypeStruct((B, S, D), F32),
        scratch_shapes=[pltpu.VMEM((N_GROUPS, HEADS, tm, HEAD_DIM), F32),
                        pltpu.VMEM((N_GROUPS, tm, LSE_LANES), F32),
                        pltpu.VMEM((tm, D), F32), pltpu.VMEM((tm, D), BF16)],
        compiler_params=pltpu.CompilerParams(
            dimension_semantics=("arbitrary",), vmem_limit_bytes=VMEM_LIMIT_BYTES),
    )(x, *outs, *lses, w_out, g_ffn, w_gu, w_down)


def _pool_ffn_kernel(x_ref, gp_ref, win_ref, wg_ref, sc_ref, gf_ref, wgu_ref, wd_ref, gn_ref,
                     out_ref, u_scr, x2_scr, h_scr, *, n_tiles, tiles_per_seq):
    tm = x_ref.shape[0]
    s = pl.program_id(0)
    i = jnp.minimum(s, n_tiles - 1) % tiles_per_seq

    @pl.when(s == 0)
    def _():
        u_scr[...] = jnp.zeros_like(u_scr)
        x2_scr[...] = jnp.zeros_like(x2_scr)
        h_scr[...] = jnp.zeros_like(h_scr)

    x3 = _ffn_from_h(x2_scr[...], h_scr[...], wgu_ref, wd_ref)
    out_ref[...] = _rms(x3, gn_ref[...])

    x = x_ref[...]
    u = _dot(_rms(x, gp_ref[...]).astype(BF16), win_ref[...])
    prev_tail = u_scr[tm:tm + POOL_HALO, :]
    u_scr[0:POOL_HALO, :] = jnp.where(i == 0, 0.0, prev_tail)
    u_scr[POOL_HALO:, :] = u
    pos = i * tm + lax.broadcasted_iota(jnp.int32, (tm, 1), 0) + 1
    ys = []
    for g, w in enumerate(POOL_WINDOWS):
        cols = slice(g * POOL_GROUP_DIM, (g + 1) * POOL_GROUP_DIM)
        wsum = u_scr[:, cols]
        shift = 1
        while shift < w:
            wsum = wsum + pltpu.roll(wsum, shift, axis=0)
            shift *= 2
        count = jnp.minimum(pos, w).astype(F32)
        y = wsum[POOL_HALO:, :] / count - u[:, cols]
        ys.append(_dot(y.astype(BF16), wg_ref[g]))
    y = jnp.concatenate(ys, axis=-1) * sc_ref[...]
    x2 = x + y
    x2_scr[...] = x2
    h_scr[...] = _rms(x2, gf_ref[...]).astype(BF16)


def _pool_ffn(x, g_pool, w_in, w_group, scale, g_ffn, w_gu, w_down, g_final):
    B, S, D = x.shape
    tm = POOL_TM
    n_tiles = B * S // tm
    in_map, out_map = _skewed_tiles(n_tiles, S // tm)
    return pl.pallas_call(
        functools.partial(_pool_ffn_kernel, n_tiles=n_tiles, tiles_per_seq=S // tm),
        name="pool_ffn",
        grid=(n_tiles + 1,),
        in_specs=[pl.BlockSpec((None, tm, D), lambda s: (*in_map(s), 0)),
                  _resident((1, D)), _resident((D, D)),
                  _resident((len(POOL_WINDOWS), POOL_GROUP_DIM, POOL_GROUP_DIM)),
                  _resident((1, D)), _resident((1, D)),
                  _resident((D, 2 * D_FF)), _resident((D_FF, D)), _resident((1, D))],
        out_specs=pl.BlockSpec((None, tm, D), lambda s: (*out_map(s), 0)),
        out_shape=jax.ShapeDtypeStruct((B, S, D), F32),
        scratch_shapes=[pltpu.VMEM((POOL_HALO + tm, D), F32),
                        pltpu.VMEM((tm, D), F32), pltpu.VMEM((tm, D), BF16)],
        compiler_params=pltpu.CompilerParams(
            dimension_semantics=("arbitrary",), vmem_limit_bytes=VMEM_LIMIT_BYTES),
    )(x, g_pool, w_in, w_group, scale, g_ffn, w_gu, w_down, g_final)


def kernel(x, attn_norm, w_qkv, w_attn_out, pool_norm, w_pool_in, w_pool_group, pool_scale,
           ffn_norm, w_ffn_gate_up, w_ffn_down, final_norm):
    assert attn_norm.shape[0] == 1 and pool_norm.shape[0] == 1 and ffn_norm.shape[0] == 2
    bf = lambda w: w.astype(BF16)
    row = lambda v: v.reshape(1, D_MODEL)

    outs, lses, qkv_far = _qkv_attn(x, row(attn_norm[0]), bf(w_qkv[0]))
    o_far, l_far = _attention(qkv_far, N_GROUPS - 1)
    x1 = _post_attn(x, (*outs, o_far), (*lses, l_far), bf(w_attn_out[0]), row(ffn_norm[0]),
                    bf(w_ffn_gate_up[0]), bf(w_ffn_down[0]))
    return _pool_ffn(x1, row(pool_norm[0]), bf(w_pool_in[0]), bf(w_pool_group[0]), row(pool_scale[0]),
                     row(ffn_norm[1]), bf(w_ffn_gate_up[1]), bf(w_ffn_down[1]), row(final_norm))
```

```python
import functools

import jax
import jax.numpy as jnp
from jax import lax
from jax.experimental import pallas as pl
from jax.experimental.pallas import tpu as pltpu

F32 = jnp.float32
BF16 = jnp.bfloat16

D_MODEL = 1024
HEADS = 8
HEAD_DIM = 128
ATTN_PAIRS = ((128, 1), (512, 4), (2048, 16))
N_GROUPS = len(ATTN_PAIRS)
GROUP_QKV = 3 * D_MODEL
Q_BLOCK = 128
POOL_WINDOWS = (2, 4, 8, 16)
POOL_GROUP_DIM = D_MODEL // len(POOL_WINDOWS)
POOL_HALO = 16
D_FF = 2816
RMS_EPS = 1e-6
LOG2_E = 1.4426950408889634
LN_2 = 0.6931471805599453
Q_SCALE = LOG2_E * HEAD_DIM ** -0.5
LANES = 128
ROW_PACK = 2
N_CHUNKS = D_MODEL // LANES
LSE_LANES = LANES

VMEM_LIMIT_BYTES = 60 * 1024 * 1024

QKV_TM = 512
ATTN_ROWS = 512
POST_TM = 512
POOL_TM = 512


def _rms(x, gamma):
    return x * lax.rsqrt(jnp.mean(x * x, axis=-1, keepdims=True) + RMS_EPS) * gamma


def _dot(a, b):
    return jnp.dot(a, b, preferred_element_type=F32)


def _swiglu_act(h, wgu_ref):
    gu = _dot(h, wgu_ref[...])
    gate, up = gu[:, :D_FF], gu[:, D_FF:]
    return (gate * jax.nn.sigmoid(gate) * up).astype(BF16)


def _pack_rows(x):
    return pltpu.bitcast(x, jnp.int32)


def _unpack_rows(x):
    return pltpu.bitcast(x, BF16)


def _resident(shape):
    return pl.BlockSpec(shape, lambda *_: (0,) * len(shape), pipeline_mode=pl.Buffered(1))


HBM_OPERAND = pl.BlockSpec(memory_space=pl.ANY)
STAGE_SLOTS = 2
STAGE_ROWS = 1024
STAGE_COLS = 256
BF16_SUBLANES = 16

STAGE_SCRATCH = (pltpu.VMEM((STAGE_SLOTS, STAGE_ROWS, STAGE_COLS), F32),
                 pltpu.SemaphoreType.DMA((STAGE_SLOTS,)))


def _fetch_weight_bf16(w_hbm, dst_ref, stage_ref, sem):
    n_rows, n_cols = dst_ref.shape
    rows = max(r for r in range(BF16_SUBLANES, STAGE_ROWS + 1, BF16_SUBLANES) if n_rows % r == 0)
    chunks = [(r0, c0) for r0 in range(0, n_rows, rows) for c0 in range(0, n_cols, STAGE_COLS)]

    def copy(c):
        r0, c0 = chunks[c]
        slot = c % STAGE_SLOTS
        return pltpu.make_async_copy(w_hbm.at[pl.ds(r0, rows), pl.ds(c0, STAGE_COLS)],
                                     stage_ref.at[slot, pl.ds(0, rows)], sem.at[slot])

    ahead = STAGE_SLOTS - 1
    for c in range(min(ahead, len(chunks))):
        copy(c).start()
    for c, (r0, c0) in enumerate(chunks):
        if c + ahead < len(chunks):
            copy(c + ahead).start()
        copy(c).wait()
        dst_ref[r0:r0 + rows, c0:c0 + STAGE_COLS] = stage_ref[c % STAGE_SLOTS, 0:rows, :].astype(BF16)


def _build_bias(bias_ref, dil, slopes):
    qi = lax.broadcasted_iota(jnp.int32, (Q_BLOCK, 2 * Q_BLOCK), 0)
    ki = lax.broadcasted_iota(jnp.int32, (Q_BLOCK, 2 * Q_BLOCK), 1)
    delta = Q_BLOCK + qi - ki
    valid = (delta >= 0) & (delta <= Q_BLOCK)
    dist = (delta * dil).astype(F32)
    for h in range(HEADS):
        bias_ref[h] = jnp.where(valid, (-slopes[h] * LOG2_E) * dist, -jnp.inf)


def _alibi_slopes(group):
    n_heads = N_GROUPS * HEADS
    return tuple(2.0 ** (-8.0 * (group * HEADS + h + 1) / n_heads) for h in range(HEADS))


def _head_cols(h, offset=0):
    return slice(offset + h * HEAD_DIM, offset + (h + 1) * HEAD_DIM)


def _attend_block(load_q, load_kv, bias_ref, first_lo, store_o, store_lse):
    ki = lax.broadcasted_iota(jnp.int32, (Q_BLOCK, 2 * Q_BLOCK), 1)
    lane = lax.broadcasted_iota(jnp.int32, (Q_BLOCK, LSE_LANES), 1)
    ones = jnp.ones((2 * Q_BLOCK, HEAD_DIM), BF16)
    m_tile = jnp.zeros((Q_BLOCK, LSE_LANES), F32)
    den_tile = jnp.ones((Q_BLOCK, LSE_LANES), F32)
    for h in range(HEADS):
        q = load_q(h)
        k, v = load_kv(h)
        t = lax.dot_general(q, k, (((1,), (1,)), ((), ())), preferred_element_type=F32)
        t = t + bias_ref[h]
        if first_lo is not None:
            t = jnp.where(ki >= first_lo, t, -jnp.inf)
        m = jnp.max(t, axis=-1, keepdims=True)
        p = jnp.exp2(t - m).astype(BF16)
        ov = _dot(p, jnp.concatenate([v, ones], axis=1))
        den = ov[:, HEAD_DIM:]
        store_o(h, (ov[:, :HEAD_DIM] / den).astype(BF16))
        m_tile = jnp.where(lane == h, m, m_tile)
        den_tile = jnp.where(lane == h, den, den_tile)
    store_lse((m_tile + jnp.log2(den_tile)) * LN_2)


FUSED_GROUPS = 2


def _qkv_attn_kernel(x_ref, g_ref, w_hbm, o1_ref, l1_ref, o2_ref, l2_ref, qkv3_ref,
                     h_scr, hp2_scr, hp3_scr, q1_scr, q2_scr, kv1_scr, kv2_scr, bias_scr,
                     w_ref, stage_scr, sem, *, slopes):
    tm = x_ref.shape[0]
    i = pl.program_id(1)

    @pl.when((pl.program_id(0) == 0) & (i == 0))
    def _():
        _fetch_weight_bf16(w_hbm.at[0], w_ref, stage_scr, sem)
        for g in range(FUSED_GROUPS):
            _build_bias(bias_scr.at[g], ATTN_PAIRS[g][1], slopes[g])
        kv1_scr[0:Q_BLOCK, :] = jnp.zeros((Q_BLOCK, 2 * D_MODEL), BF16)
        kv2_scr[:, 0:Q_BLOCK, :] = jnp.zeros((kv2_scr.shape[0], Q_BLOCK, 2 * D_MODEL), BF16)

    h = _rms(x_ref[...], g_ref[...])
    for c in range(N_CHUNKS):
        h_scr[c] = h[:, c * LANES:(c + 1) * LANES]

    def fold(hp_ref, dil):
        rows = tm // dil
        for r in range(dil):
            for c in range(N_CHUNKS):
                hp_ref[r * rows:(r + 1) * rows, c * LANES:(c + 1) * LANES] = (
                    h_scr[c, pl.ds(r, rows, stride=dil), :].astype(BF16))

    def project(hp, g):
        y = _dot(hp, w_ref[:, g * GROUP_QKV:(g + 1) * GROUP_QKV])
        return (y[:, :D_MODEL] * Q_SCALE).astype(BF16), y[:, D_MODEL:].astype(BF16)

    first_lo = jnp.where(i == 0, Q_BLOCK, 0)
    pb = Q_BLOCK // ROW_PACK

    q, kv = project(h.astype(BF16), 0)
    q1_scr[...] = q
    kv1_scr[Q_BLOCK:, :] = kv
    for j in range(tm // Q_BLOCK):
        def store_o(hd, o, j=j):
            o1_ref[0, j * pb:(j + 1) * pb, _head_cols(hd)] = _pack_rows(o)

        def store_lse(lse, j=j):
            l1_ref[0, j * Q_BLOCK:(j + 1) * Q_BLOCK, :] = lse

        keys = slice(j * Q_BLOCK, (j + 2) * Q_BLOCK)
        _attend_block(
            lambda hd, j=j: q1_scr[j * Q_BLOCK:(j + 1) * Q_BLOCK, _head_cols(hd)],
            lambda hd, keys=keys: (kv1_scr[keys, _head_cols(hd)], kv1_scr[keys, _head_cols(hd, D_MODEL)]),
            bias_scr.at[0], first_lo if j == 0 else None, store_o, store_lse)
    kv1_scr[0:Q_BLOCK, :] = kv1_scr[tm:tm + Q_BLOCK, :]

    dil = ATTN_PAIRS[1][1]
    assert tm // dil == Q_BLOCK
    fold(hp2_scr, dil)
    q, kv = project(hp2_scr[...], 1)
    q2_scr[...] = q
    for r in range(dil):
        kv2_scr[r, Q_BLOCK:, :] = kv[r * Q_BLOCK:(r + 1) * Q_BLOCK, :]
    for r in range(dil):
        def store_o(hd, o, r=r):
            o2_ref[r, :, _head_cols(hd)] = _pack_rows(o)

        def store_lse(lse, r=r):
            l2_ref[r] = lse

        _attend_block(
            lambda hd, r=r: q2_scr[r * Q_BLOCK:(r + 1) * Q_BLOCK, _head_cols(hd)],
            lambda hd, r=r: (kv2_scr[r, :, _head_cols(hd)], kv2_scr[r, :, _head_cols(hd, D_MODEL)]),
            bias_scr.at[1], first_lo, store_o, store_lse)
    for r in range(dil):
        kv2_scr[r, 0:Q_BLOCK, :] = kv2_scr[r, Q_BLOCK:, :]

    dil = ATTN_PAIRS[2][1]
    rows = tm // dil
    fold(hp3_scr, dil)
    q, kv = project(hp3_scr[...], 2)
    for r in range(dil):
        qkv3_ref[r, :, :D_MODEL] = _pack_rows(q[r * rows:(r + 1) * rows, :])
        qkv3_ref[r, :, D_MODEL:] = _pack_rows(kv[r * rows:(r + 1) * rows, :])


def _qkv_attn(x, gamma, w_qkv):
    B, S, D = x.shape
    tm = QKV_TM
    dils = [dil for _, dil in ATTN_PAIRS]

    def folded(width, rows_per_word):
        def spec(dil):
            return pl.BlockSpec((None, dil, tm // dil // rows_per_word, width), lambda b, i: (b, 0, i, 0))
        return spec

    o_spec, l_spec = folded(D, ROW_PACK), folded(LSE_LANES, 1)
    o_shape = lambda dil: jax.ShapeDtypeStruct((B, dil, S // dil // ROW_PACK, D), jnp.int32)
    l_shape = lambda dil: jax.ShapeDtypeStruct((B, dil, S // dil, LSE_LANES), F32)
    d2 = dils[2]
    o1, l1, o2, l2, qkv3 = pl.pallas_call(
        functools.partial(_qkv_attn_kernel, slopes=[_alibi_slopes(g) for g in range(FUSED_GROUPS)]),
        name="qkv_attn",
        grid=(B, S // tm),
        in_specs=[
            pl.BlockSpec((None, tm, D), lambda b, i: (b, i, 0)),
            _resident((1, D)),
            HBM_OPERAND,
        ],
        out_specs=[o_spec(dils[0]), l_spec(dils[0]), o_spec(dils[1]), l_spec(dils[1]),
                   folded(GROUP_QKV, ROW_PACK)(d2)],
        out_shape=[o_shape(dils[0]), l_shape(dils[0]), o_shape(dils[1]), l_shape(dils[1]),
                   jax.ShapeDtypeStruct((B, d2, S // d2 // ROW_PACK, GROUP_QKV), jnp.int32)],
        scratch_shapes=[
            pltpu.VMEM((N_CHUNKS, tm, LANES), F32),
            pltpu.VMEM((tm, D), BF16), pltpu.VMEM((tm, D), BF16),
            pltpu.VMEM((tm, D), BF16), pltpu.VMEM((tm, D), BF16),
            pltpu.VMEM((Q_BLOCK + tm, 2 * D), BF16),
            pltpu.VMEM((dils[1], 2 * Q_BLOCK, 2 * D), BF16),
            pltpu.VMEM((FUSED_GROUPS, HEADS, Q_BLOCK, 2 * Q_BLOCK), F32),
            pltpu.VMEM((D, N_GROUPS * GROUP_QKV), BF16),
            *STAGE_SCRATCH,
        ],
        compiler_params=pltpu.CompilerParams(
            dimension_semantics=("arbitrary", "arbitrary"), vmem_limit_bytes=VMEM_LIMIT_BYTES),
    )(x, gamma, w_qkv)
    return (o1, o2), (l1, l2), qkv3


def _attn_kernel(q_ref, k_ref, v_ref, o_ref, lse_ref, bias_scr, *, dil, slopes):
    n_res, tq, _ = lse_ref.shape

    @pl.when((pl.program_id(0) == 0) & (pl.program_id(1) == 0))
    def _():
        _build_bias(bias_scr, dil, slopes)

    pb = Q_BLOCK // ROW_PACK
    for r in range(n_res):
        for i in range(tq // Q_BLOCK):
            prows = slice(i * pb, (i + 1) * pb)
            krows = slice((i - 1) * pb, (i + 1) * pb) if i > 0 else None

            def load_kv(hd, r=r, krows=krows):
                if krows is None:
                    k = jnp.concatenate([k_ref[r, 0:pb, _head_cols(hd)]] * 2, axis=0)
                    v = jnp.concatenate([v_ref[r, 0:pb, _head_cols(hd)]] * 2, axis=0)
                else:
                    k, v = k_ref[r, krows, _head_cols(hd)], v_ref[r, krows, _head_cols(hd)]
                return _unpack_rows(k), _unpack_rows(v)

            def store_o(hd, o, r=r, prows=prows):
                o_ref[r, prows, _head_cols(hd)] = _pack_rows(o)

            def store_lse(lse, r=r, i=i):
                lse_ref[r, i * Q_BLOCK:(i + 1) * Q_BLOCK, :] = lse

            _attend_block(
                lambda hd, r=r, prows=prows: _unpack_rows(q_ref[r, prows, _head_cols(hd)]),
                load_kv, bias_scr, Q_BLOCK if i == 0 else None, store_o, store_lse)


def _attention(qkv, group):
    _, dil = ATTN_PAIRS[group]
    B, _, Lp, _ = qkv.shape
    L = Lp * ROW_PACK
    assert L <= ATTN_ROWS
    n_res = ATTN_ROWS // L

    def part(c):
        return pl.BlockSpec((None, n_res, Lp, D_MODEL), lambda b, r: (b, r, 0, c))

    return pl.pallas_call(
        functools.partial(_attn_kernel, dil=dil, slopes=_alibi_slopes(group)),
        name=f"attn_dil{dil}",
        grid=(B, dil // n_res),
        in_specs=[part(0), part(1), part(2)],
        out_specs=[
            pl.BlockSpec((None, n_res, Lp, D_MODEL), lambda b, r: (b, r, 0, 0)),
            pl.BlockSpec((None, n_res, L, LSE_LANES), lambda b, r: (b, r, 0, 0)),
        ],
        out_shape=[
            jax.ShapeDtypeStruct((B, dil, Lp, D_MODEL), jnp.int32),
            jax.ShapeDtypeStruct((B, dil, L, LSE_LANES), F32),
        ],
        scratch_shapes=[pltpu.VMEM((HEADS, Q_BLOCK, 2 * Q_BLOCK), F32)],
        compiler_params=pltpu.CompilerParams(
            dimension_semantics=("arbitrary", "arbitrary"), vmem_limit_bytes=VMEM_LIMIT_BYTES),
    )(qkv, qkv, qkv)


def _post_attn_kernel(x_ref, o1_ref, o2_ref, o3_ref, l1_ref, l2_ref, l3_ref, wo_hbm,
                      gf_ref, wgu_hbm, wd_hbm, out_ref, o_scr, l_scr, x1_scr, h_scr,
                      wo_ref, wgu_ref, wd_ref, stage_scr, sem, *, layer):
    tm = x_ref.shape[0]

    @pl.when(pl.program_id(0) == 0)
    def _():
        _fetch_weight_bf16(wo_hbm.at[0], wo_ref, stage_scr, sem)
        _fetch_weight_bf16(wgu_hbm.at[layer], wgu_ref, stage_scr, sem)
        _fetch_weight_bf16(wd_hbm.at[layer], wd_ref, stage_scr, sem)
        x1_scr[...] = jnp.zeros_like(x1_scr)
        h_scr[...] = jnp.zeros_like(h_scr)

    out_ref[...] = x1_scr[...] + _dot(_swiglu_act(h_scr[...], wgu_ref), wd_ref[...])

    o_refs = (o1_ref, o2_ref, o3_ref)
    l_refs = (l1_ref, l2_ref, l3_ref)
    for g, (_, dil) in enumerate(ATTN_PAIRS):
        rows = tm // dil
        for r in range(dil):
            l_scr[g, pl.ds(r, rows, stride=dil), :] = l_refs[g][r]
            for h in range(HEADS):
                o_scr[g, h, pl.ds(r, rows, stride=dil), :] = (
                    _unpack_rows(o_refs[g][r, :, h * HEAD_DIM:(h + 1) * HEAD_DIM]).astype(F32))
    lse = [l_scr[g] for g in range(N_GROUPS)]
    m = jnp.maximum(jnp.maximum(lse[0], lse[1]), lse[2])
    e = [jnp.exp(l - m) for l in lse]
    inv = 1.0 / (e[0] + e[1] + e[2])
    wts = [ei * inv for ei in e]
    heads = []
    for h in range(HEADS):
        acc = wts[0][:, h:h + 1] * o_scr[0, h]
        for g in range(1, N_GROUPS):
            acc = acc + wts[g][:, h:h + 1] * o_scr[g, h]
        heads.append(acc.astype(BF16))
    merged = jnp.concatenate(heads, axis=-1)
    x1 = x_ref[...] + _dot(merged, wo_ref[...])
    x1_scr[...] = x1
    h_scr[...] = _rms(x1, gf_ref[...]).astype(BF16)


def _skewed_tiles(n_tiles, tiles_per_seq):
    def in_map(s):
        t = jnp.minimum(s, n_tiles - 1)
        return t // tiles_per_seq, t % tiles_per_seq

    def out_map(s):
        t = jnp.maximum(s - 1, 0)
        return t // tiles_per_seq, t % tiles_per_seq

    return in_map, out_map


def _post_attn(x, outs, lses, w_out, g_ffn, w_gu, w_down, layer):
    B, S, D = x.shape
    tm = POST_TM
    n_tiles = B * S // tm
    in_map, out_map = _skewed_tiles(n_tiles, S // tm)

    def folded(s):
        b, i = in_map(s)
        return b, 0, i, 0

    o_specs = [pl.BlockSpec((None, dil, tm // dil // ROW_PACK, D), folded) for _, dil in ATTN_PAIRS]
    l_specs = [pl.BlockSpec((None, dil, tm // dil, LSE_LANES), folded) for _, dil in ATTN_PAIRS]
    return pl.pallas_call(
        functools.partial(_post_attn_kernel, layer=layer),
        name="post_attn",
        grid=(n_tiles + 1,),
        in_specs=[pl.BlockSpec((None, tm, D), lambda s: (*in_map(s), 0)),
                  *o_specs, *l_specs, HBM_OPERAND, _resident((1, D)), HBM_OPERAND, HBM_OPERAND],
        out_specs=pl.BlockSpec((None, tm, D), lambda s: (*out_map(s), 0)),
        out_shape=jax.ShapeDtypeStruct((B, S, D), F32),
        scratch_shapes=[pltpu.VMEM((N_GROUPS, HEADS, tm, HEAD_DIM), F32),
                        pltpu.VMEM((N_GROUPS, tm, LSE_LANES), F32),
                        pltpu.VMEM((tm, D), F32), pltpu.VMEM((tm, D), BF16),
                        pltpu.VMEM((D, D), BF16), pltpu.VMEM((D, 2 * D_FF), BF16),
                        pltpu.VMEM((D_FF, D), BF16), *STAGE_SCRATCH],
        compiler_params=pltpu.CompilerParams(
            dimension_semantics=("arbitrary",), vmem_limit_bytes=VMEM_LIMIT_BYTES),
    )(x, *outs, *lses, w_out, g_ffn, w_gu, w_down)


def _pool_ffn_kernel(x_ref, gp_ref, win_hbm, wg_hbm, sc_ref, gf_ref, wgu_hbm, wd_hbm, gn_ref,
                     out_ref, u_scr, x2_scr, h_scr, win_ref, wg_ref, wgu_ref, wd_ref, stage_scr, sem,
                     *, n_tiles, tiles_per_seq, layer):
    tm = x_ref.shape[0]
    s = pl.program_id(0)
    i = jnp.minimum(s, n_tiles - 1) % tiles_per_seq

    @pl.when(s == 0)
    def _():
        _fetch_weight_bf16(win_hbm.at[0], win_ref, stage_scr, sem)
        for g in range(len(POOL_WINDOWS)):
            _fetch_weight_bf16(wg_hbm.at[0, g], wg_ref.at[g], stage_scr, sem)
        _fetch_weight_bf16(wgu_hbm.at[layer], wgu_ref, stage_scr, sem)
        _fetch_weight_bf16(wd_hbm.at[layer], wd_ref, stage_scr, sem)
        u_scr[...] = jnp.zeros_like(u_scr)
        x2_scr[...] = jnp.zeros_like(x2_scr)
        h_scr[...] = jnp.zeros_like(h_scr)

    x = x_ref[...]
    u = _dot(_rms(x, gp_ref[...]).astype(BF16), win_ref[...])
    act = _swiglu_act(h_scr[...], wgu_ref)
    prev_tail = u_scr[tm:tm + POOL_HALO, :]
    u_scr[0:POOL_HALO, :] = jnp.where(i == 0, 0.0, prev_tail)
    u_scr[POOL_HALO:, :] = u
    pos = i * tm + lax.broadcasted_iota(jnp.int32, (tm, 1), 0) + 1
    ys = []
    for g, w in enumerate(POOL_WINDOWS):
        cols = slice(g * POOL_GROUP_DIM, (g + 1) * POOL_GROUP_DIM)
        wsum = u_scr[:, cols]
        shift = 1
        while shift < w:
            wsum = wsum + pltpu.roll(wsum, shift, axis=0)
            shift *= 2
        count = jnp.minimum(pos, w).astype(F32)
        y = wsum[POOL_HALO:, :] / count - u[:, cols]
        ys.append(_dot(y.astype(BF16), wg_ref[g]))
    y = jnp.concatenate(ys, axis=-1) * sc_ref[...]
    x2 = x + y
    x3 = x2_scr[...] + _dot(act, wd_ref[...])
    out_ref[...] = _rms(x3, gn_ref[...])
    x2_scr[...] = x2
    h_scr[...] = _rms(x2, gf_ref[...]).astype(BF16)


def _pool_ffn(x, g_pool, w_in, w_group, scale, g_ffn, w_gu, w_down, g_final, layer):
    B, S, D = x.shape
    tm = POOL_TM
    n_tiles = B * S // tm
    in_map, out_map = _skewed_tiles(n_tiles, S // tm)
    return pl.pallas_call(
        functools.partial(_pool_ffn_kernel, n_tiles=n_tiles, tiles_per_seq=S // tm, layer=layer),
        name="pool_ffn",
        grid=(n_tiles + 1,),
        in_specs=[pl.BlockSpec((None, tm, D), lambda s: (*in_map(s), 0)),
                  _resident((1, D)), HBM_OPERAND, HBM_OPERAND,
                  _resident((1, D)), _resident((1, D)),
                  HBM_OPERAND, HBM_OPERAND, _resident((1, D))],
        out_specs=pl.BlockSpec((None, tm, D), lambda s: (*out_map(s), 0)),
        out_shape=jax.ShapeDtypeStruct((B, S, D), F32),
        scratch_shapes=[pltpu.VMEM((POOL_HALO + tm, D), F32),
                        pltpu.VMEM((tm, D), F32), pltpu.VMEM((tm, D), BF16),
                        pltpu.VMEM((D, D), BF16),
                        pltpu.VMEM((len(POOL_WINDOWS), POOL_GROUP_DIM, POOL_GROUP_DIM), BF16),
                        pltpu.VMEM((D, 2 * D_FF), BF16), pltpu.VMEM((D_FF, D), BF16), *STAGE_SCRATCH],
        compiler_params=pltpu.CompilerParams(
            dimension_semantics=("arbitrary",), vmem_limit_bytes=VMEM_LIMIT_BYTES),
    )(x, g_pool, w_in, w_group, scale, g_ffn, w_gu, w_down, g_final)


def kernel(x, attn_norm, w_qkv, w_attn_out, pool_norm, w_pool_in, w_pool_group, pool_scale,
           ffn_norm, w_ffn_gate_up, w_ffn_down, final_norm):
    assert attn_norm.shape[0] == 1 and pool_norm.shape[0] == 1 and ffn_norm.shape[0] == 2
    row = lambda v: v.reshape(1, D_MODEL)

    outs, lses, qkv_far = _qkv_attn(x, row(attn_norm[0]), w_qkv)
    o_far, l_far = _attention(qkv_far, N_GROUPS - 1)
    x1 = _post_attn(x, (*outs, o_far), (*lses, l_far), w_attn_out, row(ffn_norm[0]),
                    w_ffn_gate_up, w_ffn_down, layer=0)
    return _pool_ffn(x1, row(pool_norm[0]), w_pool_in, w_pool_group, row(pool_scale[0]),
                     row(ffn_norm[1]), w_ffn_gate_up, w_ffn_down, row(final_norm), layer=1)
```

```python
import functools

import jax
import jax.numpy as jnp
from jax import lax
from jax.experimental import pallas as pl
from jax.experimental.pallas import tpu as pltpu

F32 = jnp.float32
BF16 = jnp.bfloat16

D_MODEL = 1024
HEADS = 8
HEAD_DIM = 128
ATTN_PAIRS = ((128, 1), (512, 4), (2048, 16))
N_GROUPS = len(ATTN_PAIRS)
GROUP_QKV = 3 * D_MODEL
Q_BLOCK = 128
POOL_WINDOWS = (2, 4, 8, 16)
POOL_GROUP_DIM = D_MODEL // len(POOL_WINDOWS)
POOL_HALO = 16
D_FF = 2816
RMS_EPS = 1e-6
LOG2_E = 1.4426950408889634
LN_2 = 0.6931471805599453
Q_SCALE = LOG2_E * HEAD_DIM ** -0.5
LANES = 128
ROW_PACK = 2
N_CHUNKS = D_MODEL // LANES
LSE_LANES = LANES

VMEM_LIMIT_BYTES = 60 * 1024 * 1024

QKV_TM = 512
ATTN_ROWS = 1024
POST_TM = 512
POOL_TM = 512


def _rms(x, gamma):
    return x * lax.rsqrt(jnp.mean(x * x, axis=-1, keepdims=True) + RMS_EPS) * gamma


def _dot(a, b):
    return jnp.dot(a, b, preferred_element_type=F32)


def _swiglu_act(h, wgu_ref):
    gu = _dot(h, wgu_ref[...])
    gate, up = gu[:, :D_FF], gu[:, D_FF:]
    return (gate * jax.nn.sigmoid(gate) * up).astype(BF16)


def _pack_rows(x):
    return pltpu.bitcast(x, jnp.int32)


def _unpack_rows(x):
    return pltpu.bitcast(x, BF16)


def _resident(shape):
    return pl.BlockSpec(shape, lambda *_: (0,) * len(shape), pipeline_mode=pl.Buffered(1))


HBM_OPERAND = pl.BlockSpec(memory_space=pl.ANY)
STAGE_SLOTS, STAGE_ROWS, STAGE_COLS = N_CHUNKS, QKV_TM, LANES
STAGE_SHAPE = (STAGE_SLOTS, STAGE_ROWS, STAGE_COLS)
STAGE_SEMAPHORES = pltpu.SemaphoreType.DMA((STAGE_SLOTS,))
BF16_SUBLANES = 16


def _fetch_weight_bf16(w_hbm, dst_ref, stage_ref, sem):
    assert stage_ref.shape == STAGE_SHAPE and stage_ref.dtype == F32
    n_rows, n_cols = dst_ref.shape
    rows = max(r for r in range(BF16_SUBLANES, STAGE_ROWS + 1, BF16_SUBLANES) if n_rows % r == 0)
    chunks = [(r0, c0) for r0 in range(0, n_rows, rows) for c0 in range(0, n_cols, STAGE_COLS)]

    def copy(c):
        r0, c0 = chunks[c]
        slot = c % STAGE_SLOTS
        return pltpu.make_async_copy(w_hbm.at[pl.ds(r0, rows), pl.ds(c0, STAGE_COLS)],
                                     stage_ref.at[slot, pl.ds(0, rows)], sem.at[slot])

    ahead = STAGE_SLOTS - 1
    for c in range(min(ahead, len(chunks))):
        copy(c).start()
    for c, (r0, c0) in enumerate(chunks):
        if c + ahead < len(chunks):
            copy(c + ahead).start()
        copy(c).wait()
        dst_ref[r0:r0 + rows, c0:c0 + STAGE_COLS] = stage_ref[c % STAGE_SLOTS, 0:rows, :].astype(BF16)


def _build_bias(bias_ref, dil, slopes):
    qi = lax.broadcasted_iota(jnp.int32, (Q_BLOCK, 2 * Q_BLOCK), 0)
    ki = lax.broadcasted_iota(jnp.int32, (Q_BLOCK, 2 * Q_BLOCK), 1)
    delta = Q_BLOCK + qi - ki
    valid = (delta >= 0) & (delta <= Q_BLOCK)
    dist = (delta * dil).astype(F32)
    for h in range(HEADS):
        bias_ref[h] = jnp.where(valid, (-slopes[h] * LOG2_E) * dist, -jnp.inf)


def _alibi_slopes(group):
    n_heads = N_GROUPS * HEADS
    return tuple(2.0 ** (-8.0 * (group * HEADS + h + 1) / n_heads) for h in range(HEADS))


def _head_cols(h, offset=0):
    return slice(offset + h * HEAD_DIM, offset + (h + 1) * HEAD_DIM)


def _attend_block(load_q, load_kv, bias_ref, first_lo, store_o, store_lse):
    ki = lax.broadcasted_iota(jnp.int32, (Q_BLOCK, 2 * Q_BLOCK), 1)
    lane = lax.broadcasted_iota(jnp.int32, (Q_BLOCK, LSE_LANES), 1)
    ones = jnp.ones((2 * Q_BLOCK, HEAD_DIM), BF16)
    m_tile = jnp.zeros((Q_BLOCK, LSE_LANES), F32)
    den_tile = jnp.ones((Q_BLOCK, LSE_LANES), F32)
    for h in range(HEADS):
        q = load_q(h)
        k, v = load_kv(h)
        t = lax.dot_general(q, k, (((1,), (1,)), ((), ())), preferred_element_type=F32)
        t = t + bias_ref[h]
        if first_lo is not None:
            t = jnp.where(ki >= first_lo, t, -jnp.inf)
        m = jnp.max(t, axis=-1, keepdims=True)
        p = jnp.exp2(t - m).astype(BF16)
        ov = _dot(p, jnp.concatenate([v, ones], axis=1))
        den = ov[:, HEAD_DIM:]
        store_o(h, (ov[:, :HEAD_DIM] / den).astype(BF16))
        m_tile = jnp.where(lane == h, m, m_tile)
        den_tile = jnp.where(lane == h, den, den_tile)
    store_lse((m_tile + jnp.log2(den_tile)) * LN_2)


FUSED_GROUPS = 2


def _qkv_attn_kernel(x_ref, g_ref, w_hbm, o1_ref, l1_ref, o2_ref, l2_ref, qkv3_ref,
                     h_scr, hp2_scr, hp3_scr, q1_scr, q2_scr, kv1_scr, kv2_scr, bias_scr,
                     w_ref, sem, *, slopes):
    tm = x_ref.shape[0]
    i = pl.program_id(1)

    @pl.when((pl.program_id(0) == 0) & (i == 0))
    def _():
        _fetch_weight_bf16(w_hbm.at[0], w_ref, h_scr, sem)
        for g in range(FUSED_GROUPS):
            _build_bias(bias_scr.at[g], ATTN_PAIRS[g][1], slopes[g])
        kv1_scr[0:Q_BLOCK, :] = jnp.zeros((Q_BLOCK, 2 * D_MODEL), BF16)
        kv2_scr[:, 0:Q_BLOCK, :] = jnp.zeros((kv2_scr.shape[0], Q_BLOCK, 2 * D_MODEL), BF16)

    h = _rms(x_ref[...], g_ref[...])
    for c in range(N_CHUNKS):
        h_scr[c] = h[:, c * LANES:(c + 1) * LANES]

    def fold(hp_ref, dil):
        rows = tm // dil
        for r in range(dil):
            for c in range(N_CHUNKS):
                hp_ref[r * rows:(r + 1) * rows, c * LANES:(c + 1) * LANES] = (
                    h_scr[c, pl.ds(r, rows, stride=dil), :].astype(BF16))

    def project(hp, g):
        y = _dot(hp, w_ref[:, g * GROUP_QKV:(g + 1) * GROUP_QKV])
        return (y[:, :D_MODEL] * Q_SCALE).astype(BF16), y[:, D_MODEL:].astype(BF16)

    first_lo = jnp.where(i == 0, Q_BLOCK, 0)
    pb = Q_BLOCK // ROW_PACK

    q, kv = project(h.astype(BF16), 0)
    q1_scr[...] = q
    kv1_scr[Q_BLOCK:, :] = kv
    for j in range(tm // Q_BLOCK):
        def store_o(hd, o, j=j):
            o1_ref[0, j * pb:(j + 1) * pb, _head_cols(hd)] = _pack_rows(o)

        def store_lse(lse, j=j):
            l1_ref[0, j * Q_BLOCK:(j + 1) * Q_BLOCK, :] = lse

        keys = slice(j * Q_BLOCK, (j + 2) * Q_BLOCK)
        _attend_block(
            lambda hd, j=j: q1_scr[j * Q_BLOCK:(j + 1) * Q_BLOCK, _head_cols(hd)],
            lambda hd, keys=keys: (kv1_scr[keys, _head_cols(hd)], kv1_scr[keys, _head_cols(hd, D_MODEL)]),
            bias_scr.at[0], first_lo if j == 0 else None, store_o, store_lse)
    kv1_scr[0:Q_BLOCK, :] = kv1_scr[tm:tm + Q_BLOCK, :]

    dil = ATTN_PAIRS[1][1]
    assert tm // dil == Q_BLOCK
    fold(hp2_scr, dil)
    q, kv = project(hp2_scr[...], 1)
    q2_scr[...] = q
    for r in range(dil):
        kv2_scr[r, Q_BLOCK:, :] = kv[r * Q_BLOCK:(r + 1) * Q_BLOCK, :]
    for r in range(dil):
        def store_o(hd, o, r=r):
            o2_ref[r, :, _head_cols(hd)] = _pack_rows(o)

        def store_lse(lse, r=r):
            l2_ref[r] = lse

        _attend_block(
            lambda hd, r=r: q2_scr[r * Q_BLOCK:(r + 1) * Q_BLOCK, _head_cols(hd)],
            lambda hd, r=r: (kv2_scr[r, :, _head_cols(hd)], kv2_scr[r, :, _head_cols(hd, D_MODEL)]),
            bias_scr.at[1], first_lo, store_o, store_lse)
    for r in range(dil):
        kv2_scr[r, 0:Q_BLOCK, :] = kv2_scr[r, Q_BLOCK:, :]

    dil = ATTN_PAIRS[2][1]
    rows = tm // dil
    fold(hp3_scr, dil)
    q, kv = project(hp3_scr[...], 2)
    for r in range(dil):
        qkv3_ref[r, :, :D_MODEL] = _pack_rows(q[r * rows:(r + 1) * rows, :])
        qkv3_ref[r, :, D_MODEL:] = _pack_rows(kv[r * rows:(r + 1) * rows, :])


def _qkv_attn(x, gamma, w_qkv):
    B, S, D = x.shape
    tm = QKV_TM
    dils = [dil for _, dil in ATTN_PAIRS]

    def folded(width, rows_per_word):
        def spec(dil):
            return pl.BlockSpec((None, dil, tm // dil // rows_per_word, width), lambda b, i: (b, 0, i, 0))
        return spec

    o_spec, l_spec = folded(D, ROW_PACK), folded(LSE_LANES, 1)
    o_shape = lambda dil: jax.ShapeDtypeStruct((B, dil, S // dil // ROW_PACK, D), jnp.int32)
    l_shape = lambda dil: jax.ShapeDtypeStruct((B, dil, S // dil, LSE_LANES), F32)
    d2 = dils[2]
    o1, l1, o2, l2, qkv3 = pl.pallas_call(
        functools.partial(_qkv_attn_kernel, slopes=[_alibi_slopes(g) for g in range(FUSED_GROUPS)]),
        name="qkv_attn",
        grid=(B, S // tm),
        in_specs=[
            pl.BlockSpec((None, tm, D), lambda b, i: (b, i, 0)),
            _resident((1, D)),
            HBM_OPERAND,
        ],
        out_specs=[o_spec(dils[0]), l_spec(dils[0]), o_spec(dils[1]), l_spec(dils[1]),
                   folded(GROUP_QKV, ROW_PACK)(d2)],
        out_shape=[o_shape(dils[0]), l_shape(dils[0]), o_shape(dils[1]), l_shape(dils[1]),
                   jax.ShapeDtypeStruct((B, d2, S // d2 // ROW_PACK, GROUP_QKV), jnp.int32)],
        scratch_shapes=[
            pltpu.VMEM((N_CHUNKS, tm, LANES), F32),
            pltpu.VMEM((tm, D), BF16), pltpu.VMEM((tm, D), BF16),
            pltpu.VMEM((tm, D), BF16), pltpu.VMEM((tm, D), BF16),
            pltpu.VMEM((Q_BLOCK + tm, 2 * D), BF16),
            pltpu.VMEM((dils[1], 2 * Q_BLOCK, 2 * D), BF16),
            pltpu.VMEM((FUSED_GROUPS, HEADS, Q_BLOCK, 2 * Q_BLOCK), F32),
            pltpu.VMEM((D, N_GROUPS * GROUP_QKV), BF16),
            STAGE_SEMAPHORES,
        ],
        compiler_params=pltpu.CompilerParams(
            dimension_semantics=("arbitrary", "arbitrary"), vmem_limit_bytes=VMEM_LIMIT_BYTES),
    )(x, gamma, w_qkv)
    return (o1, o2), (l1, l2), qkv3


def _attn_kernel(q_ref, k_ref, v_ref, o_ref, lse_ref, bias_scr, *, dil, slopes):
    n_res, tq, _ = lse_ref.shape

    @pl.when((pl.program_id(0) == 0) & (pl.program_id(1) == 0))
    def _():
        _build_bias(bias_scr, dil, slopes)

    pb = Q_BLOCK // ROW_PACK
    for r in range(n_res):
        for i in range(tq // Q_BLOCK):
            prows = slice(i * pb, (i + 1) * pb)
            krows = slice((i - 1) * pb, (i + 1) * pb) if i > 0 else None

            def load_kv(hd, r=r, krows=krows):
                if krows is None:
                    k = jnp.concatenate([k_ref[r, 0:pb, _head_cols(hd)]] * 2, axis=0)
                    v = jnp.concatenate([v_ref[r, 0:pb, _head_cols(hd)]] * 2, axis=0)
                else:
                    k, v = k_ref[r, krows, _head_cols(hd)], v_ref[r, krows, _head_cols(hd)]
                return _unpack_rows(k), _unpack_rows(v)

            def store_o(hd, o, r=r, prows=prows):
                o_ref[r, prows, _head_cols(hd)] = _pack_rows(o)

            def store_lse(lse, r=r, i=i):
                lse_ref[r, i * Q_BLOCK:(i + 1) * Q_BLOCK, :] = lse

            _attend_block(
                lambda hd, r=r, prows=prows: _unpack_rows(q_ref[r, prows, _head_cols(hd)]),
                load_kv, bias_scr, Q_BLOCK if i == 0 else None, store_o, store_lse)


def _attention(qkv, group):
    _, dil = ATTN_PAIRS[group]
    B, _, Lp, _ = qkv.shape
    L = Lp * ROW_PACK
    assert L <= ATTN_ROWS
    n_res = ATTN_ROWS // L

    def part(c):
        return pl.BlockSpec((None, n_res, Lp, D_MODEL), lambda b, r: (b, r, 0, c))

    return pl.pallas_call(
        functools.partial(_attn_kernel, dil=dil, slopes=_alibi_slopes(group)),
        name=f"attn_dil{dil}",
        grid=(B, dil // n_res),
        in_specs=[part(0), part(1), part(2)],
        out_specs=[
            pl.BlockSpec((None, n_res, Lp, D_MODEL), lambda b, r: (b, r, 0, 0)),
            pl.BlockSpec((None, n_res, L, LSE_LANES), lambda b, r: (b, r, 0, 0)),
        ],
        out_shape=[
            jax.ShapeDtypeStruct((B, dil, Lp, D_MODEL), jnp.int32),
            jax.ShapeDtypeStruct((B, dil, L, LSE_LANES), F32),
        ],
        scratch_shapes=[pltpu.VMEM((HEADS, Q_BLOCK, 2 * Q_BLOCK), F32)],
        compiler_params=pltpu.CompilerParams(
            dimension_semantics=("arbitrary", "arbitrary"), vmem_limit_bytes=VMEM_LIMIT_BYTES),
    )(qkv, qkv, qkv)


def _post_attn_kernel(x_ref, o1_ref, o2_ref, o3_ref, l1_ref, l2_ref, l3_ref, wo_hbm,
                      gf_ref, wgu_hbm, wd_hbm, out_ref, o_scr, l_scr, x1_scr, h_scr,
                      wo_ref, wgu_ref, wd_ref, sem, *, layer):
    tm = x_ref.shape[0]

    @pl.when(pl.program_id(0) == 0)
    def _():
        stage = o_scr.at[0]
        _fetch_weight_bf16(wo_hbm.at[0], wo_ref, stage, sem)
        _fetch_weight_bf16(wgu_hbm.at[layer], wgu_ref, stage, sem)
        _fetch_weight_bf16(wd_hbm.at[layer], wd_ref, stage, sem)
        x1_scr[...] = jnp.zeros_like(x1_scr)
        h_scr[...] = jnp.zeros_like(h_scr)

    out_ref[...] = x1_scr[...] + _dot(_swiglu_act(h_scr[...], wgu_ref), wd_ref[...])

    o_refs = (o1_ref, o2_ref, o3_ref)
    l_refs = (l1_ref, l2_ref, l3_ref)
    for g, (_, dil) in enumerate(ATTN_PAIRS):
        rows = tm // dil
        for r in range(dil):
            l_scr[g, pl.ds(r, rows, stride=dil), :] = l_refs[g][r]
            for h in range(HEADS):
                o_scr[g, h, pl.ds(r, rows, stride=dil), :] = (
                    _unpack_rows(o_refs[g][r, :, h * HEAD_DIM:(h + 1) * HEAD_DIM]).astype(F32))
    lse = [l_scr[g] for g in range(N_GROUPS)]
    m = jnp.maximum(jnp.maximum(lse[0], lse[1]), lse[2])
    e = [jnp.exp(l - m) for l in lse]
    inv = 1.0 / (e[0] + e[1] + e[2])
    wts = [ei * inv for ei in e]
    heads = []
    for h in range(HEADS):
        acc = wts[0][:, h:h + 1] * o_scr[0, h]
        for g in range(1, N_GROUPS):
            acc = acc + wts[g][:, h:h + 1] * o_scr[g, h]
        heads.append(acc.astype(BF16))
    merged = jnp.concatenate(heads, axis=-1)
    x1 = x_ref[...] + _dot(merged, wo_ref[...])
    x1_scr[...] = x1
    h_scr[...] = _rms(x1, gf_ref[...]).astype(BF16)


def _skewed_tiles(n_tiles, tiles_per_seq):
    def in_map(s):
        t = jnp.minimum(s, n_tiles - 1)
        return t // tiles_per_seq, t % tiles_per_seq

    def out_map(s):
        t = jnp.maximum(s - 1, 0)
        return t // tiles_per_seq, t % tiles_per_seq

    return in_map, out_map


def _post_attn(x, outs, lses, w_out, g_ffn, w_gu, w_down, layer):
    B, S, D = x.shape
    tm = POST_TM
    n_tiles = B * S // tm
    in_map, out_map = _skewed_tiles(n_tiles, S // tm)

    def folded(s):
        b, i = in_map(s)
        return b, 0, i, 0

    o_specs = [pl.BlockSpec((None, dil, tm // dil // ROW_PACK, D), folded) for _, dil in ATTN_PAIRS]
    l_specs = [pl.BlockSpec((None, dil, tm // dil, LSE_LANES), folded) for _, dil in ATTN_PAIRS]
    return pl.pallas_call(
        functools.partial(_post_attn_kernel, layer=layer),
        name="post_attn",
        grid=(n_tiles + 1,),
        in_specs=[pl.BlockSpec((None, tm, D), lambda s: (*in_map(s), 0)),
                  *o_specs, *l_specs, HBM_OPERAND, _resident((1, D)), HBM_OPERAND, HBM_OPERAND],
        out_specs=pl.BlockSpec((None, tm, D), lambda s: (*out_map(s), 0)),
        out_shape=jax.ShapeDtypeStruct((B, S, D), F32),
        scratch_shapes=[pltpu.VMEM((N_GROUPS, HEADS, tm, HEAD_DIM), F32),
                        pltpu.VMEM((N_GROUPS, tm, LSE_LANES), F32),
                        pltpu.VMEM((tm, D), F32), pltpu.VMEM((tm, D), BF16),
                        pltpu.VMEM((D, D), BF16), pltpu.VMEM((D, 2 * D_FF), BF16),
                        pltpu.VMEM((D_FF, D), BF16), STAGE_SEMAPHORES],
        compiler_params=pltpu.CompilerParams(
            dimension_semantics=("arbitrary",), vmem_limit_bytes=VMEM_LIMIT_BYTES),
    )(x, *outs, *lses, w_out, g_ffn, w_gu, w_down)


def _pool_ffn_kernel(x_ref, gp_ref, win_hbm, wg_hbm, sc_ref, gf_ref, wgu_hbm, wd_hbm, gn_ref,
                     out_ref, u_scr, x2_scr, h_scr, win_ref, wg_ref, wgu_ref, wd_ref, stage_scr, sem,
                     *, n_tiles, tiles_per_seq, layer):
    tm = x_ref.shape[0]
    s = pl.program_id(0)
    i = jnp.minimum(s, n_tiles - 1) % tiles_per_seq

    @pl.when(s == 0)
    def _():
        _fetch_weight_bf16(win_hbm.at[0], win_ref, stage_scr, sem)
        for g in range(len(POOL_WINDOWS)):
            _fetch_weight_bf16(wg_hbm.at[0, g], wg_ref.at[g], stage_scr, sem)
        _fetch_weight_bf16(wgu_hbm.at[layer], wgu_ref, stage_scr, sem)
        _fetch_weight_bf16(wd_hbm.at[layer], wd_ref, stage_scr, sem)
        u_scr[...] = jnp.zeros_like(u_scr)
        x2_scr[...] = jnp.zeros_like(x2_scr)
        h_scr[...] = jnp.zeros_like(h_scr)

    x = x_ref[...]
    u = _dot(_rms(x, gp_ref[...]).astype(BF16), win_ref[...])
    act = _swiglu_act(h_scr[...], wgu_ref)
    prev_tail = u_scr[tm:tm + POOL_HALO, :]
    u_scr[0:POOL_HALO, :] = jnp.where(i == 0, 0.0, prev_tail)
    u_scr[POOL_HALO:, :] = u
    pos = i * tm + lax.broadcasted_iota(jnp.int32, (tm, 1), 0) + 1
    ys = []
    for g, w in enumerate(POOL_WINDOWS):
        cols = slice(g * POOL_GROUP_DIM, (g + 1) * POOL_GROUP_DIM)
        wsum = u_scr[:, cols]
        shift = 1
        while shift < w:
            wsum = wsum + pltpu.roll(wsum, shift, axis=0)
            shift *= 2
        count = jnp.minimum(pos, w).astype(F32)
        y = wsum[POOL_HALO:, :] / count - u[:, cols]
        ys.append(_dot(y.astype(BF16), wg_ref[g]))
    y = jnp.concatenate(ys, axis=-1) * sc_ref[...]
    x2 = x + y
    x3 = x2_scr[...] + _dot(act, wd_ref[...])
    out_ref[...] = _rms(x3, gn_ref[...])
    x2_scr[...] = x2
    h_scr[...] = _rms(x2, gf_ref[...]).astype(BF16)


def _pool_ffn(x, g_pool, w_in, w_group, scale, g_ffn, w_gu, w_down, g_final, layer):
    B, S, D = x.shape
    tm = POOL_TM
    n_tiles = B * S // tm
    in_map, out_map = _skewed_tiles(n_tiles, S // tm)
    return pl.pallas_call(
        functools.partial(_pool_ffn_kernel, n_tiles=n_tiles, tiles_per_seq=S // tm, layer=layer),
        name="pool_ffn",
        grid=(n_tiles + 1,),
        in_specs=[pl.BlockSpec((None, tm, D), lambda s: (*in_map(s), 0)),
                  _resident((1, D)), HBM_OPERAND, HBM_OPERAND,
                  _resident((1, D)), _resident((1, D)),
                  HBM_OPERAND, HBM_OPERAND, _resident((1, D))],
        out_specs=pl.BlockSpec((None, tm, D), lambda s: (*out_map(s), 0)),
        out_shape=jax.ShapeDtypeStruct((B, S, D), F32),
        scratch_shapes=[pltpu.VMEM((POOL_HALO + tm, D), F32),
                        pltpu.VMEM((tm, D), F32), pltpu.VMEM((tm, D), BF16),
                        pltpu.VMEM((D, D), BF16),
                        pltpu.VMEM((len(POOL_WINDOWS), POOL_GROUP_DIM, POOL_GROUP_DIM), BF16),
                        pltpu.VMEM((D, 2 * D_FF), BF16), pltpu.VMEM((D_FF, D), BF16),
                        pltpu.VMEM(STAGE_SHAPE, F32), STAGE_SEMAPHORES],
        compiler_params=pltpu.CompilerParams(
            dimension_semantics=("arbitrary",), vmem_limit_bytes=VMEM_LIMIT_BYTES),
    )(x, g_pool, w_in, w_group, scale, g_ffn, w_gu, w_down, g_final)


def kernel(x, attn_norm, w_qkv, w_attn_out, pool_norm, w_pool_in, w_pool_group, pool_scale,
           ffn_norm, w_ffn_gate_up, w_ffn_down, final_norm):
    assert attn_norm.shape[0] == 1 and pool_norm.shape[0] == 1 and ffn_norm.shape[0] == 2
    row = lambda v: v.reshape(1, D_MODEL)

    outs, lses, qkv_far = _qkv_attn(x, row(attn_norm[0]), w_qkv)
    o_far, l_far = _attention(qkv_far, N_GROUPS - 1)
    x1 = _post_attn(x, (*outs, o_far), (*lses, l_far), w_attn_out, row(ffn_norm[0]),
                    w_ffn_gate_up, w_ffn_down, layer=0)
    return _pool_ffn(x1, row(pool_norm[0]), w_pool_in, w_pool_group, row(pool_scale[0]),
                     row(ffn_norm[1]), w_ffn_gate_up, w_ffn_down, row(final_norm), layer=1)
```

```python
import functools

import jax
import jax.numpy as jnp
from jax import lax
from jax.experimental import pallas as pl
from jax.experimental.pallas import tpu as pltpu

F32 = jnp.float32
BF16 = jnp.bfloat16

D_MODEL = 1024
HEADS = 8
HEAD_DIM = 128
ATTN_PAIRS = ((128, 1), (512, 4), (2048, 16))
N_GROUPS = len(ATTN_PAIRS)
GROUP_QKV = 3 * D_MODEL
Q_BLOCK = 128
POOL_WINDOWS = (2, 4, 8, 16)
POOL_GROUP_DIM = D_MODEL // len(POOL_WINDOWS)
POOL_HALO = 16
D_FF = 2816
RMS_EPS = 1e-6
LOG2_E = 1.4426950408889634
LN_2 = 0.6931471805599453
Q_SCALE = LOG2_E * HEAD_DIM ** -0.5
LANES = 128
ROW_PACK = 2
N_CHUNKS = D_MODEL // LANES
LSE_LANES = LANES

VMEM_LIMIT_BYTES = 60 * 1024 * 1024

QKV_TM = 512
ATTN_ROWS = 1024
POST_TM = 512
POOL_TM = 512


def _rms(x, gamma):
    return x * lax.rsqrt(jnp.mean(x * x, axis=-1, keepdims=True) + RMS_EPS) * gamma


def _dot(a, b):
    return jnp.dot(a, b, preferred_element_type=F32)


def _swiglu_act(h, wgu_ref):
    gu = _dot(h, wgu_ref[...])
    gate, up = gu[:, :D_FF], gu[:, D_FF:]
    return (gate * jax.nn.sigmoid(gate) * up).astype(BF16)


def _pack_rows(x):
    return pltpu.bitcast(x, jnp.int32)


def _unpack_rows(x):
    return pltpu.bitcast(x, BF16)


def _resident(shape):
    return pl.BlockSpec(shape, lambda *_: (0,) * len(shape), pipeline_mode=pl.Buffered(1))


HBM_OPERAND = pl.BlockSpec(memory_space=pl.ANY)
STAGE_SLOTS = 4
STAGE_SLOT_BYTES = 640 * 1024
STAGE_SEMAPHORES = pltpu.SemaphoreType.DMA((STAGE_SLOTS,))
BF16_SUBLANES = 16


def _stage_scratch(n_rows, n_cols):
    rows = BF16_SUBLANES
    while n_rows % (2 * rows) == 0 and 2 * rows * n_cols * 4 <= STAGE_SLOT_BYTES:
        rows *= 2
    return pltpu.VMEM((STAGE_SLOTS, rows, n_cols), F32)


def _fetch_weight_bf16(w_hbm, dst_ref, stage_ref, sem):
    n_rows, n_cols = dst_ref.shape
    n_slots, rows, stage_cols = stage_ref.shape
    assert n_slots == STAGE_SLOTS and stage_cols == n_cols and n_rows % rows == 0
    n_chunks = n_rows // rows

    def copy(c):
        slot = c % STAGE_SLOTS
        return pltpu.make_async_copy(w_hbm.at[pl.ds(c * rows, rows), :], stage_ref.at[slot], sem.at[slot])

    ahead = STAGE_SLOTS - 1
    for c in range(min(ahead, n_chunks)):
        copy(c).start()
    for c in range(n_chunks):
        if c + ahead < n_chunks:
            copy(c + ahead).start()
        copy(c).wait()
        dst_ref[c * rows:(c + 1) * rows, :] = stage_ref[c % STAGE_SLOTS].astype(BF16)


def _build_bias(bias_ref, dil, slopes):
    qi = lax.broadcasted_iota(jnp.int32, (Q_BLOCK, 2 * Q_BLOCK), 0)
    ki = lax.broadcasted_iota(jnp.int32, (Q_BLOCK, 2 * Q_BLOCK), 1)
    delta = Q_BLOCK + qi - ki
    valid = (delta >= 0) & (delta <= Q_BLOCK)
    dist = (delta * dil).astype(F32)
    for h in range(HEADS):
        bias_ref[h] = jnp.where(valid, (-slopes[h] * LOG2_E) * dist, -jnp.inf)


def _alibi_slopes(group):
    n_heads = N_GROUPS * HEADS
    return tuple(2.0 ** (-8.0 * (group * HEADS + h + 1) / n_heads) for h in range(HEADS))


def _head_cols(h, offset=0):
    return slice(offset + h * HEAD_DIM, offset + (h + 1) * HEAD_DIM)


def _attend_block(load_q, load_kv, bias_ref, first_lo, store_o, store_lse):
    ki = lax.broadcasted_iota(jnp.int32, (Q_BLOCK, 2 * Q_BLOCK), 1)
    lane = lax.broadcasted_iota(jnp.int32, (Q_BLOCK, LSE_LANES), 1)
    ones = jnp.ones((2 * Q_BLOCK, HEAD_DIM), BF16)
    m_tile = jnp.zeros((Q_BLOCK, LSE_LANES), F32)
    den_tile = jnp.ones((Q_BLOCK, LSE_LANES), F32)
    for h in range(HEADS):
        q = load_q(h)
        k, v = load_kv(h)
        t = lax.dot_general(q, k, (((1,), (1,)), ((), ())), preferred_element_type=F32)
        t = t + bias_ref[h]
        if first_lo is not None:
            t = jnp.where(ki >= first_lo, t, -jnp.inf)
        m = jnp.max(t, axis=-1, keepdims=True)
        p = jnp.exp2(t - m).astype(BF16)
        ov = _dot(p, jnp.concatenate([v, ones], axis=1))
        den = ov[:, HEAD_DIM:]
        store_o(h, (ov[:, :HEAD_DIM] / den).astype(BF16))
        m_tile = jnp.where(lane == h, m, m_tile)
        den_tile = jnp.where(lane == h, den, den_tile)
    store_lse((m_tile + jnp.log2(den_tile)) * LN_2)


FUSED_GROUPS = 2


def _qkv_attn_kernel(x_ref, g_ref, w_hbm, o1_ref, l1_ref, o2_ref, l2_ref, qkv3_ref,
                     h_scr, hp2_scr, hp3_scr, q1_scr, q2_scr, kv1_scr, kv2_scr, bias_scr,
                     w_ref, stage_scr, sem, *, slopes):
    tm = x_ref.shape[0]
    i = pl.program_id(1)

    @pl.when((pl.program_id(0) == 0) & (i == 0))
    def _():
        _fetch_weight_bf16(w_hbm.at[0], w_ref, stage_scr, sem)
        for g in range(FUSED_GROUPS):
            _build_bias(bias_scr.at[g], ATTN_PAIRS[g][1], slopes[g])
        kv1_scr[0:Q_BLOCK, :] = jnp.zeros((Q_BLOCK, 2 * D_MODEL), BF16)
        kv2_scr[:, 0:Q_BLOCK, :] = jnp.zeros((kv2_scr.shape[0], Q_BLOCK, 2 * D_MODEL), BF16)

    h = _rms(x_ref[...], g_ref[...])
    for c in range(N_CHUNKS):
        h_scr[c] = h[:, c * LANES:(c + 1) * LANES]

    def fold(hp_ref, dil):
        rows = tm // dil
        for r in range(dil):
            for c in range(N_CHUNKS):
                hp_ref[r * rows:(r + 1) * rows, c * LANES:(c + 1) * LANES] = (
                    h_scr[c, pl.ds(r, rows, stride=dil), :].astype(BF16))

    def project(hp, g):
        y = _dot(hp, w_ref[:, g * GROUP_QKV:(g + 1) * GROUP_QKV])
        return (y[:, :D_MODEL] * Q_SCALE).astype(BF16), y[:, D_MODEL:].astype(BF16)

    first_lo = jnp.where(i == 0, Q_BLOCK, 0)
    pb = Q_BLOCK // ROW_PACK

    q, kv = project(h.astype(BF16), 0)
    q1_scr[...] = q
    kv1_scr[Q_BLOCK:, :] = kv
    for j in range(tm // Q_BLOCK):
        def store_o(hd, o, j=j):
            o1_ref[0, j * pb:(j + 1) * pb, _head_cols(hd)] = _pack_rows(o)

        def store_lse(lse, j=j):
            l1_ref[0, j * Q_BLOCK:(j + 1) * Q_BLOCK, :] = lse

        keys = slice(j * Q_BLOCK, (j + 2) * Q_BLOCK)
        _attend_block(
            lambda hd, j=j: q1_scr[j * Q_BLOCK:(j + 1) * Q_BLOCK, _head_cols(hd)],
            lambda hd, keys=keys: (kv1_scr[keys, _head_cols(hd)], kv1_scr[keys, _head_cols(hd, D_MODEL)]),
            bias_scr.at[0], first_lo if j == 0 else None, store_o, store_lse)
    kv1_scr[0:Q_BLOCK, :] = kv1_scr[tm:tm + Q_BLOCK, :]

    dil = ATTN_PAIRS[1][1]
    assert tm // dil == Q_BLOCK
    fold(hp2_scr, dil)
    q, kv = project(hp2_scr[...], 1)
    q2_scr[...] = q
    for r in range(dil):
        kv2_scr[r, Q_BLOCK:, :] = kv[r * Q_BLOCK:(r + 1) * Q_BLOCK, :]
    for r in range(dil):
        def store_o(hd, o, r=r):
            o2_ref[r, :, _head_cols(hd)] = _pack_rows(o)

        def store_lse(lse, r=r):
            l2_ref[r] = lse

        _attend_block(
            lambda hd, r=r: q2_scr[r * Q_BLOCK:(r + 1) * Q_BLOCK, _head_cols(hd)],
            lambda hd, r=r: (kv2_scr[r, :, _head_cols(hd)], kv2_scr[r, :, _head_cols(hd, D_MODEL)]),
            bias_scr.at[1], first_lo, store_o, store_lse)
    for r in range(dil):
        kv2_scr[r, 0:Q_BLOCK, :] = kv2_scr[r, Q_BLOCK:, :]

    dil = ATTN_PAIRS[2][1]
    rows = tm // dil
    fold(hp3_scr, dil)
    q, kv = project(hp3_scr[...], 2)
    for r in range(dil):
        qkv3_ref[r, :, :D_MODEL] = _pack_rows(q[r * rows:(r + 1) * rows, :])
        qkv3_ref[r, :, D_MODEL:] = _pack_rows(kv[r * rows:(r + 1) * rows, :])

def _qkv_attn(x, gamma, w_qkv):
    B, S, D = x.shape
    tm = QKV_TM
    dils = [dil for _, dil in ATTN_PAIRS]

    def folded(width, rows_per_word):
        def spec(dil):
            return pl.BlockSpec((None, dil, tm // dil // rows_per_word, width), lambda b, i: (b, 0, i, 0))
        return spec

    o_spec, l_spec = folded(D, ROW_PACK), folded(LSE_LANES, 1)
    o_shape = lambda dil: jax.ShapeDtypeStruct((B, dil, S // dil // ROW_PACK, D), jnp.int32)
    l_shape = lambda dil: jax.ShapeDtypeStruct((B, dil, S // dil, LSE_LANES), F32)
    d2 = dils[2]
    o1, l1, o2, l2, qkv3 = pl.pallas_call(
        functools.partial(_qkv_attn_kernel, slopes=[_alibi_slopes(g) for g in range(FUSED_GROUPS)]),
        name="qkv_attn",
        grid=(B, S // tm),
        in_specs=[
            pl.BlockSpec((None, tm, D), lambda b, i: (b, i, 0)),
            _resident((1, D)),
            HBM_OPERAND,
        ],
        out_specs=[o_spec(dils[0]), l_spec(dils[0]), o_spec(dils[1]), l_spec(dils[1]),
                   folded(GROUP_QKV, ROW_PACK)(d2)],
        out_shape=[o_shape(dils[0]), l_shape(dils[0]), o_shape(dils[1]), l_shape(dils[1]),
                   jax.ShapeDtypeStruct((B, d2, S // d2 // ROW_PACK, GROUP_QKV), jnp.int32)],
        scratch_shapes=[
            pltpu.VMEM((N_CHUNKS, tm, LANES), F32),
            pltpu.VMEM((tm, D), BF16), pltpu.VMEM((tm, D), BF16),
            pltpu.VMEM((tm, D), BF16), pltpu.VMEM((tm, D), BF16),
            pltpu.VMEM((Q_BLOCK + tm, 2 * D), BF16),
            pltpu.VMEM((dils[1], 2 * Q_BLOCK, 2 * D), BF16),
            pltpu.VMEM((FUSED_GROUPS, HEADS, Q_BLOCK, 2 * Q_BLOCK), F32),
            pltpu.VMEM((D, N_GROUPS * GROUP_QKV), BF16),
            _stage_scratch(D, N_GROUPS * GROUP_QKV), STAGE_SEMAPHORES,
        ],
        compiler_params=pltpu.CompilerParams(
            dimension_semantics=("arbitrary", "arbitrary"), vmem_limit_bytes=VMEM_LIMIT_BYTES),
    )(x, gamma, w_qkv)
    return (o1, o2), (l1, l2), qkv3


def _attn_kernel(q_ref, k_ref, v_ref, o_ref, lse_ref, bias_scr, *, dil, slopes):
    n_res, tq, _ = lse_ref.shape

    @pl.when((pl.program_id(0) == 0) & (pl.program_id(1) == 0))
    def _():
        _build_bias(bias_scr, dil, slopes)

    pb = Q_BLOCK // ROW_PACK
    for r in range(n_res):
        for i in range(tq // Q_BLOCK):
            prows = slice(i * pb, (i + 1) * pb)
            krows = slice((i - 1) * pb, (i + 1) * pb) if i > 0 else None

            def load_kv(hd, r=r, krows=krows):
                if krows is None:
                    k = jnp.concatenate([k_ref[r, 0:pb, _head_cols(hd)]] * 2, axis=0)
                    v = jnp.concatenate([v_ref[r, 0:pb, _head_cols(hd)]] * 2, axis=0)
                else:
                    k, v = k_ref[r, krows, _head_cols(hd)], v_ref[r, krows, _head_cols(hd)]
                return _unpack_rows(k), _unpack_rows(v)

            def store_o(hd, o, r=r, prows=prows):
                o_ref[r, prows, _head_cols(hd)] = _pack_rows(o)

            def store_lse(lse, r=r, i=i):
                lse_ref[r, i * Q_BLOCK:(i + 1) * Q_BLOCK, :] = lse

            _attend_block(
                lambda hd, r=r, prows=prows: _unpack_rows(q_ref[r, prows, _head_cols(hd)]),
                load_kv, bias_scr, Q_BLOCK if i == 0 else None, store_o, store_lse)


def _attention(qkv, group):
    _, dil = ATTN_PAIRS[group]
    B, _, Lp, _ = qkv.shape
    L = Lp * ROW_PACK
    assert L <= ATTN_ROWS
    n_res = ATTN_ROWS // L

    def part(c):
        return pl.BlockSpec((None, n_res, Lp, D_MODEL), lambda b, r: (b, r, 0, c))

    return pl.pallas_call(
        functools.partial(_attn_kernel, dil=dil, slopes=_alibi_slopes(group)),
        name=f"attn_dil{dil}",
        grid=(B, dil // n_res),
        in_specs=[part(0), part(1), part(2)],
        out_specs=[
            pl.BlockSpec((None, n_res, Lp, D_MODEL), lambda b, r: (b, r, 0, 0)),
            pl.BlockSpec((None, n_res, L, LSE_LANES), lambda b, r: (b, r, 0, 0)),
        ],
        out_shape=[
            jax.ShapeDtypeStruct((B, dil, Lp, D_MODEL), jnp.int32),
            jax.ShapeDtypeStruct((B, dil, L, LSE_LANES), F32),
        ],
        scratch_shapes=[pltpu.VMEM((HEADS, Q_BLOCK, 2 * Q_BLOCK), F32)],
        compiler_params=pltpu.CompilerParams(
            dimension_semantics=("arbitrary", "arbitrary"), vmem_limit_bytes=VMEM_LIMIT_BYTES),
    )(qkv, qkv, qkv)


def _post_attn_kernel(x_ref, o1_ref, o2_ref, o3_ref, l1_ref, l2_ref, l3_ref, wo_hbm,
                      gf_ref, wgu_hbm, wd_hbm, out_ref, o_scr, l_scr, x1_scr, h_scr,
                      wo_ref, wgu_ref, wd_ref, stage_d, stage_f, sem, *, layer):
    tm = x_ref.shape[0]

    @pl.when(pl.program_id(0) == 0)
    def _():
        _fetch_weight_bf16(wo_hbm.at[0], wo_ref, stage_d, sem)
        _fetch_weight_bf16(wgu_hbm.at[layer], wgu_ref, stage_f, sem)
        _fetch_weight_bf16(wd_hbm.at[layer], wd_ref, stage_d, sem)
        x1_scr[...] = jnp.zeros_like(x1_scr)
        h_scr[...] = jnp.zeros_like(h_scr)

    out_ref[...] = x1_scr[...] + _dot(_swiglu_act(h_scr[...], wgu_ref), wd_ref[...])

    o_refs = (o1_ref, o2_ref, o3_ref)
    l_refs = (l1_ref, l2_ref, l3_ref)
    for g, (_, dil) in enumerate(ATTN_PAIRS):
        rows = tm // dil
        for r in range(dil):
            l_scr[g, pl.ds(r, rows, stride=dil), :] = l_refs[g][r]
            for h in range(HEADS):
                o_scr[g, h, pl.ds(r, rows, stride=dil), :] = (
                    _unpack_rows(o_refs[g][r, :, h * HEAD_DIM:(h + 1) * HEAD_DIM]).astype(F32))
    lse = [l_scr[g] for g in range(N_GROUPS)]
    m = jnp.maximum(jnp.maximum(lse[0], lse[1]), lse[2])
    e = [jnp.exp(l - m) for l in lse]
    inv = 1.0 / (e[0] + e[1] + e[2])
    wts = [ei * inv for ei in e]
    heads = []
    for h in range(HEADS):
        acc = wts[0][:, h:h + 1] * o_scr[0, h]
        for g in range(1, N_GROUPS):
            acc = acc + wts[g][:, h:h + 1] * o_scr[g, h]
        heads.append(acc.astype(BF16))
    merged = jnp.concatenate(heads, axis=-1)
    x1 = x_ref[...] + _dot(merged, wo_ref[...])
    x1_scr[...] = x1
    h_scr[...] = _rms(x1, gf_ref[...]).astype(BF16)


def _skewed_tiles(n_tiles, tiles_per_seq):
    def in_map(s):
        t = jnp.minimum(s, n_tiles - 1)
        return t // tiles_per_seq, t % tiles_per_seq

    def out_map(s):
        t = jnp.maximum(s - 1, 0)
        return t // tiles_per_seq, t % tiles_per_seq

    return in_map, out_map


def _post_attn(x, outs, lses, w_out, g_ffn, w_gu, w_down, layer):
    B, S, D = x.shape
    tm = POST_TM
    n_tiles = B * S // tm
    in_map, out_map = _skewed_tiles(n_tiles, S // tm)

    def folded(s):
        b, i = in_map(s)
        return b, 0, i, 0

    o_specs = [pl.BlockSpec((None, dil, tm // dil // ROW_PACK, D), folded) for _, dil in ATTN_PAIRS]
    l_specs = [pl.BlockSpec((None, dil, tm // dil, LSE_LANES), folded) for _, dil in ATTN_PAIRS]
    return pl.pallas_call(
        functools.partial(_post_attn_kernel, layer=layer),
        name="post_attn",
        grid=(n_tiles + 1,),
        in_specs=[pl.BlockSpec((None, tm, D), lambda s: (*in_map(s), 0)),
                  *o_specs, *l_specs, HBM_OPERAND, _resident((1, D)), HBM_OPERAND, HBM_OPERAND],
        out_specs=pl.BlockSpec((None, tm, D), lambda s: (*out_map(s), 0)),
        out_shape=jax.ShapeDtypeStruct((B, S, D), F32),
        scratch_shapes=[pltpu.VMEM((N_GROUPS, HEADS, tm, HEAD_DIM), F32),
                        pltpu.VMEM((N_GROUPS, tm, LSE_LANES), F32),
                        pltpu.VMEM((tm, D), F32), pltpu.VMEM((tm, D), BF16),
                        pltpu.VMEM((D, D), BF16), pltpu.VMEM((D, 2 * D_FF), BF16),
                        pltpu.VMEM((D_FF, D), BF16),
                        _stage_scratch(D_FF, D), _stage_scratch(D, 2 * D_FF), STAGE_SEMAPHORES],
        compiler_params=pltpu.CompilerParams(
            dimension_semantics=("arbitrary",), vmem_limit_bytes=VMEM_LIMIT_BYTES),
    )(x, *outs, *lses, w_out, g_ffn, w_gu, w_down)


def _pool_ffn_kernel(x_ref, gp_ref, win_hbm, wg_hbm, sc_ref, gf_ref, wgu_hbm, wd_hbm, gn_ref,
                     out_ref, u_scr, x2_scr, h_scr, win_ref, wg_ref, wgu_ref, wd_ref,
                     stage_d, stage_f, stage_g, sem,
                     *, n_tiles, tiles_per_seq, layer):
    tm = x_ref.shape[0]
    s = pl.program_id(0)
    i = jnp.minimum(s, n_tiles - 1) % tiles_per_seq

    @pl.when(s == 0)
    def _():
        _fetch_weight_bf16(win_hbm.at[0], win_ref, stage_d, sem)
        for g in range(len(POOL_WINDOWS)):
            _fetch_weight_bf16(wg_hbm.at[0, g], wg_ref.at[g], stage_g, sem)
        _fetch_weight_bf16(wgu_hbm.at[layer], wgu_ref, stage_f, sem)
        _fetch_weight_bf16(wd_hbm.at[layer], wd_ref, stage_d, sem)
        u_scr[...] = jnp.zeros_like(u_scr)
        x2_scr[...] = jnp.zeros_like(x2_scr)
        h_scr[...] = jnp.zeros_like(h_scr)

    x = x_ref[...]
    u = _dot(_rms(x, gp_ref[...]).astype(BF16), win_ref[...])
    act = _swiglu_act(h_scr[...], wgu_ref)
    prev_tail = u_scr[tm:tm + POOL_HALO, :]
    u_scr[0:POOL_HALO, :] = jnp.where(i == 0, 0.0, prev_tail)
    u_scr[POOL_HALO:, :] = u
    pos = i * tm + lax.broadcasted_iota(jnp.int32, (tm, 1), 0) + 1
    ys = []
    for g, w in enumerate(POOL_WINDOWS):
        cols = slice(g * POOL_GROUP_DIM, (g + 1) * POOL_GROUP_DIM)
        wsum = u_scr[:, cols]
        shift = 1
        while shift < w:
            wsum = wsum + pltpu.roll(wsum, shift, axis=0)
            shift *= 2
        count = jnp.minimum(pos, w).astype(F32)
        y = wsum[POOL_HALO:, :] / count - u[:, cols]
        ys.append(_dot(y.astype(BF16), wg_ref[g]))
    y = jnp.concatenate(ys, axis=-1) * sc_ref[...]
    x2 = x + y
    x3 = x2_scr[...] + _dot(act, wd_ref[...])
    out_ref[...] = _rms(x3, gn_ref[...])
    x2_scr[...] = x2
    h_scr[...] = _rms(x2, gf_ref[...]).astype(BF16)


def _pool_ffn(x, g_pool, w_in, w_group, scale, g_ffn, w_gu, w_down, g_final, layer):
    B, S, D = x.shape
    tm = POOL_TM
    n_tiles = B * S // tm
    in_map, out_map = _skewed_tiles(n_tiles, S // tm)
    return pl.pallas_call(
        functools.partial(_pool_ffn_kernel, n_tiles=n_tiles, tiles_per_seq=S // tm, layer=layer),
        name="pool_ffn",
        grid=(n_tiles + 1,),
        in_specs=[pl.BlockSpec((None, tm, D), lambda s: (*in_map(s), 0)),
                  _resident((1, D)), HBM_OPERAND, HBM_OPERAND,
                  _resident((1, D)), _resident((1, D)),
                  HBM_OPERAND, HBM_OPERAND, _resident((1, D))],
        out_specs=pl.BlockSpec((None, tm, D), lambda s: (*out_map(s), 0)),
        out_shape=jax.ShapeDtypeStruct((B, S, D), F32),
        scratch_shapes=[pltpu.VMEM((POOL_HALO + tm, D), F32),
                        pltpu.VMEM((tm, D), F32), pltpu.VMEM((tm, D), BF16),
                        pltpu.VMEM((D, D), BF16),
                        pltpu.VMEM((len(POOL_WINDOWS), POOL_GROUP_DIM, POOL_GROUP_DIM), BF16),
                        pltpu.VMEM((D, 2 * D_FF), BF16), pltpu.VMEM((D_FF, D), BF16),
                        _stage_scratch(D_FF, D), _stage_scratch(D, 2 * D_FF),
                        _stage_scratch(POOL_GROUP_DIM, POOL_GROUP_DIM), STAGE_SEMAPHORES],
        compiler_params=pltpu.CompilerParams(
            dimension_semantics=("arbitrary",), vmem_limit_bytes=VMEM_LIMIT_BYTES),
    )(x, g_pool, w_in, w_group, scale, g_ffn, w_gu, w_down, g_final)


def kernel(x, attn_norm, w_qkv, w_attn_out, pool_norm, w_pool_in, w_pool_group, pool_scale,
           ffn_norm, w_ffn_gate_up, w_ffn_down, final_norm):
    assert attn_norm.shape[0] == 1 and pool_norm.shape[0] == 1 and ffn_norm.shape[0] == 2
    row = lambda v: v.reshape(1, D_MODEL)

    outs, lses, qkv_far = _qkv_attn(x, row(attn_norm[0]), w_qkv)
    o_far, l_far = _attention(qkv_far, N_GROUPS - 1)
    x1 = _post_attn(x, (*outs, o_far), (*lses, l_far), w_attn_out, row(ffn_norm[0]),
                    w_ffn_gate_up, w_ffn_down, layer=0)
    return _pool_ffn(x1, row(pool_norm[0]), w_pool_in, w_pool_group, row(pool_scale[0]),
                     row(ffn_norm[1]), w_ffn_gate_up, w_ffn_down, row(final_norm), layer=1)
```

```python
import functools

import jax
import jax.numpy as jnp
from jax import lax
from jax.experimental import pallas as pl
from jax.experimental.pallas import tpu as pltpu

F32 = jnp.float32
BF16 = jnp.bfloat16

D_MODEL = 1024
HEADS = 8
HEAD_DIM = 128
ATTN_PAIRS = ((128, 1), (512, 4), (2048, 16))
N_GROUPS = len(ATTN_PAIRS)
GROUP_QKV = 3 * D_MODEL
Q_BLOCK = 128
POOL_WINDOWS = (2, 4, 8, 16)
POOL_GROUP_DIM = D_MODEL // len(POOL_WINDOWS)
POOL_HALO = 16
D_FF = 2816
RMS_EPS = 1e-6
LOG2_E = 1.4426950408889634
LN_2 = 0.6931471805599453
Q_SCALE = LOG2_E * HEAD_DIM ** -0.5
LANES = 128
ROW_PACK = 2
N_CHUNKS = D_MODEL // LANES
LSE_LANES = LANES

VMEM_LIMIT_BYTES = 60 * 1024 * 1024

QKV_TM = 512
ATTN_ROWS = 2048
POST_TM = 512
POOL_TM = 512


def _rms(x, gamma):
    return x * lax.rsqrt(jnp.mean(x * x, axis=-1, keepdims=True) + RMS_EPS) * gamma


def _dot(a, b):
    return jnp.dot(a, b, preferred_element_type=F32)


def _swiglu_act(h, wgu_ref):
    gu = _dot(h, wgu_ref[...])
    gate, up = gu[:, :D_FF], gu[:, D_FF:]
    return (gate * jax.nn.sigmoid(gate) * up).astype(BF16)


def _pack_rows(x):
    return pltpu.bitcast(x, jnp.int32)


def _unpack_rows(x):
    return pltpu.bitcast(x, BF16)


def _resident(shape):
    return pl.BlockSpec(shape, lambda *_: (0,) * len(shape), pipeline_mode=pl.Buffered(1))


HBM_OPERAND = pl.BlockSpec(memory_space=pl.ANY)
STAGE_SLOTS = 4
STAGE_SLOT_BYTES = 640 * 1024
STAGE_SEMAPHORES = pltpu.SemaphoreType.DMA((STAGE_SLOTS,))
BF16_SUBLANES = 16


def _stage_scratch(n_rows, n_cols):
    rows = BF16_SUBLANES
    while n_rows % (2 * rows) == 0 and 2 * rows * n_cols * 4 <= STAGE_SLOT_BYTES:
        rows *= 2
    return pltpu.VMEM((STAGE_SLOTS, rows, n_cols), F32)


def _fetch_weight_bf16(w_hbm, dst_ref, stage_ref, sem):
    n_rows, n_cols = dst_ref.shape
    n_slots, rows, stage_cols = stage_ref.shape
    assert n_slots == STAGE_SLOTS and stage_cols == n_cols and n_rows % rows == 0
    n_chunks = n_rows // rows

    def copy(c):
        slot = c % STAGE_SLOTS
        return pltpu.make_async_copy(w_hbm.at[pl.ds(c * rows, rows), :], stage_ref.at[slot], sem.at[slot])

    ahead = STAGE_SLOTS - 1
    for c in range(min(ahead, n_chunks)):
        copy(c).start()
    for c in range(n_chunks):
        if c + ahead < n_chunks:
            copy(c + ahead).start()
        copy(c).wait()
        dst_ref[c * rows:(c + 1) * rows, :] = stage_ref[c % STAGE_SLOTS].astype(BF16)


def _build_bias(bias_ref, dil, slopes):
    qi = lax.broadcasted_iota(jnp.int32, (Q_BLOCK, 2 * Q_BLOCK), 0)
    ki = lax.broadcasted_iota(jnp.int32, (Q_BLOCK, 2 * Q_BLOCK), 1)
    delta = Q_BLOCK + qi - ki
    valid = (delta >= 0) & (delta <= Q_BLOCK)
    dist = (delta * dil).astype(F32)
    for h in range(HEADS):
        bias_ref[h] = jnp.where(valid, (-slopes[h] * LOG2_E) * dist, -jnp.inf)


def _alibi_slopes(group):
    n_heads = N_GROUPS * HEADS
    return tuple(2.0 ** (-8.0 * (group * HEADS + h + 1) / n_heads) for h in range(HEADS))


def _head_cols(h, offset=0):
    return slice(offset + h * HEAD_DIM, offset + (h + 1) * HEAD_DIM)


def _attend_block(load_q, load_kv, bias_ref, first_lo, store_o, store_lse):
    ki = lax.broadcasted_iota(jnp.int32, (Q_BLOCK, 2 * Q_BLOCK), 1)
    lane = lax.broadcasted_iota(jnp.int32, (Q_BLOCK, LSE_LANES), 1)
    ones = jnp.ones((2 * Q_BLOCK, HEAD_DIM), BF16)
    m_tile = jnp.zeros((Q_BLOCK, LSE_LANES), F32)
    den_tile = jnp.ones((Q_BLOCK, LSE_LANES), F32)
    for h in range(HEADS):
        q = load_q(h)
        k, v = load_kv(h)
        t = lax.dot_general(q, k, (((1,), (1,)), ((), ())), preferred_element_type=F32)
        t = t + bias_ref[h]
        if first_lo is not None:
            t = jnp.where(ki >= first_lo, t, -jnp.inf)
        m = jnp.max(t, axis=-1, keepdims=True)
        p = jnp.exp2(t - m).astype(BF16)
        ov = _dot(p, jnp.concatenate([v, ones], axis=1))
        den = ov[:, HEAD_DIM:]
        store_o(h, (ov[:, :HEAD_DIM] / den).astype(BF16))
        m_tile = jnp.where(lane == h, m, m_tile)
        den_tile = jnp.where(lane == h, den, den_tile)
    store_lse((m_tile + jnp.log2(den_tile)) * LN_2)


FUSED_GROUPS = 2


def _qkv_attn_kernel(x_ref, g_ref, w_hbm, o1_ref, l1_ref, o2_ref, l2_ref, qkv3_ref,
                     h_scr, hp2_scr, hp3_scr, q1_scr, q2_scr, kv1_scr, kv2_scr, bias_scr,
                     w_ref, stage_scr, sem, *, slopes):
    tm = x_ref.shape[0]
    i = pl.program_id(1)

    @pl.when((pl.program_id(0) == 0) & (i == 0))
    def _():
        _fetch_weight_bf16(w_hbm.at[0], w_ref, stage_scr, sem)
        for g in range(FUSED_GROUPS):
            _build_bias(bias_scr.at[g], ATTN_PAIRS[g][1], slopes[g])
        kv1_scr[0:Q_BLOCK, :] = jnp.zeros((Q_BLOCK, 2 * D_MODEL), BF16)
        kv2_scr[:, 0:Q_BLOCK, :] = jnp.zeros((kv2_scr.shape[0], Q_BLOCK, 2 * D_MODEL), BF16)

    h = _rms(x_ref[...], g_ref[...])
    for c in range(N_CHUNKS):
        h_scr[c] = h[:, c * LANES:(c + 1) * LANES]

    def fold(hp_ref, dil):
        rows = tm // dil
        for r in range(dil):
            for c in range(N_CHUNKS):
                hp_ref[r * rows:(r + 1) * rows, c * LANES:(c + 1) * LANES] = (
                    h_scr[c, pl.ds(r, rows, stride=dil), :].astype(BF16))

    def project(hp, g):
        y = _dot(hp, w_ref[:, g * GROUP_QKV:(g + 1) * GROUP_QKV])
        return (y[:, :D_MODEL] * Q_SCALE).astype(BF16), y[:, D_MODEL:].astype(BF16)

    first_lo = jnp.where(i == 0, Q_BLOCK, 0)
    pb = Q_BLOCK // ROW_PACK

    q, kv = project(h.astype(BF16), 0)
    q1_scr[...] = q
    kv1_scr[Q_BLOCK:, :] = kv
    for j in range(tm // Q_BLOCK):
        def store_o(hd, o, j=j):
            o1_ref[0, j * pb:(j + 1) * pb, _head_cols(hd)] = _pack_rows(o)

        def store_lse(lse, j=j):
            l1_ref[0, j * Q_BLOCK:(j + 1) * Q_BLOCK, :] = lse

        keys = slice(j * Q_BLOCK, (j + 2) * Q_BLOCK)
        _attend_block(
            lambda hd, j=j: q1_scr[j * Q_BLOCK:(j + 1) * Q_BLOCK, _head_cols(hd)],
            lambda hd, keys=keys: (kv1_scr[keys, _head_cols(hd)], kv1_scr[keys, _head_cols(hd, D_MODEL)]),
            bias_scr.at[0], first_lo if j == 0 else None, store_o, store_lse)
    kv1_scr[0:Q_BLOCK, :] = kv1_scr[tm:tm + Q_BLOCK, :]

    dil = ATTN_PAIRS[1][1]
    assert tm // dil == Q_BLOCK
    fold(hp2_scr, dil)
    q, kv = project(hp2_scr[...], 1)
    q2_scr[...] = q
    for r in range(dil):
        kv2_scr[r, Q_BLOCK:, :] = kv[r * Q_BLOCK:(r + 1) * Q_BLOCK, :]
    for r in range(dil):
        def store_o(hd, o, r=r):
            o2_ref[r, :, _head_cols(hd)] = _pack_rows(o)

        def store_lse(lse, r=r):
            l2_ref[r] = lse

        _attend_block(
            lambda hd, r=r: q2_scr[r * Q_BLOCK:(r + 1) * Q_BLOCK, _head_cols(hd)],
            lambda hd, r=r: (kv2_scr[r, :, _head_cols(hd)], kv2_scr[r, :, _head_cols(hd, D_MODEL)]),
            bias_scr.at[1], first_lo, store_o, store_lse)
    for r in range(dil):
        kv2_scr[r, 0:Q_BLOCK, :] = kv2_scr[r, Q_BLOCK:, :]

    dil = ATTN_PAIRS[2][1]
    rows = tm // dil
    fold(hp3_scr, dil)
    q, kv = project(hp3_scr[...], 2)
    for r in range(dil):
        qkv3_ref[r, :, :D_MODEL] = _pack_rows(q[r * rows:(r + 1) * rows, :])
        qkv3_ref[r, :, D_MODEL:] = _pack_rows(kv[r * rows:(r + 1) * rows, :])

def _qkv_attn(x, gamma, w_qkv):
    B, S, D = x.shape
    tm = QKV_TM
    dils = [dil for _, dil in ATTN_PAIRS]

    def folded(width, rows_per_word):
        def spec(dil):
            return pl.BlockSpec((None, dil, tm // dil // rows_per_word, width), lambda b, i: (b, 0, i, 0))
        return spec

    o_spec, l_spec = folded(D, ROW_PACK), folded(LSE_LANES, 1)
    o_shape = lambda dil: jax.ShapeDtypeStruct((B, dil, S // dil // ROW_PACK, D), jnp.int32)
    l_shape = lambda dil: jax.ShapeDtypeStruct((B, dil, S // dil, LSE_LANES), F32)
    d2 = dils[2]
    o1, l1, o2, l2, qkv3 = pl.pallas_call(
        functools.partial(_qkv_attn_kernel, slopes=[_alibi_slopes(g) for g in range(FUSED_GROUPS)]),
        name="qkv_attn",
        grid=(B, S // tm),
        in_specs=[
            pl.BlockSpec((None, tm, D), lambda b, i: (b, i, 0)),
            _resident((1, D)),
            HBM_OPERAND,
        ],
        out_specs=[o_spec(dils[0]), l_spec(dils[0]), o_spec(dils[1]), l_spec(dils[1]),
                   folded(GROUP_QKV, ROW_PACK)(d2)],
        out_shape=[o_shape(dils[0]), l_shape(dils[0]), o_shape(dils[1]), l_shape(dils[1]),
                   jax.ShapeDtypeStruct((B, d2, S // d2 // ROW_PACK, GROUP_QKV), jnp.int32)],
        scratch_shapes=[
            pltpu.VMEM((N_CHUNKS, tm, LANES), F32),
            pltpu.VMEM((tm, D), BF16), pltpu.VMEM((tm, D), BF16),
            pltpu.VMEM((tm, D), BF16), pltpu.VMEM((tm, D), BF16),
            pltpu.VMEM((Q_BLOCK + tm, 2 * D), BF16),
            pltpu.VMEM((dils[1], 2 * Q_BLOCK, 2 * D), BF16),
            pltpu.VMEM((FUSED_GROUPS, HEADS, Q_BLOCK, 2 * Q_BLOCK), F32),
            pltpu.VMEM((D, N_GROUPS * GROUP_QKV), BF16),
            _stage_scratch(D, N_GROUPS * GROUP_QKV), STAGE_SEMAPHORES,
        ],
        compiler_params=pltpu.CompilerParams(
            dimension_semantics=("arbitrary", "arbitrary"), vmem_limit_bytes=VMEM_LIMIT_BYTES),
    )(x, gamma, w_qkv)
    return (o1, o2), (l1, l2), qkv3


def _attn_kernel(q_ref, k_ref, v_ref, o_ref, lse_ref, bias_scr, *, dil, slopes):
    n_res, tq, _ = lse_ref.shape

    @pl.when((pl.program_id(0) == 0) & (pl.program_id(1) == 0))
    def _():
        _build_bias(bias_scr, dil, slopes)

    pb = Q_BLOCK // ROW_PACK
    for r in range(n_res):
        for i in range(tq // Q_BLOCK):
            prows = slice(i * pb, (i + 1) * pb)
            krows = slice((i - 1) * pb, (i + 1) * pb) if i > 0 else None

            def load_kv(hd, r=r, krows=krows):
                if krows is None:
                    k = jnp.concatenate([k_ref[r, 0:pb, _head_cols(hd)]] * 2, axis=0)
                    v = jnp.concatenate([v_ref[r, 0:pb, _head_cols(hd)]] * 2, axis=0)
                else:
                    k, v = k_ref[r, krows, _head_cols(hd)], v_ref[r, krows, _head_cols(hd)]
                return _unpack_rows(k), _unpack_rows(v)

            def store_o(hd, o, r=r, prows=prows):
                o_ref[r, prows, _head_cols(hd)] = _pack_rows(o)

            def store_lse(lse, r=r, i=i):
                lse_ref[r, i * Q_BLOCK:(i + 1) * Q_BLOCK, :] = lse

            _attend_block(
                lambda hd, r=r, prows=prows: _unpack_rows(q_ref[r, prows, _head_cols(hd)]),
                load_kv, bias_scr, Q_BLOCK if i == 0 else None, store_o, store_lse)


def _attention(qkv, group):
    _, dil = ATTN_PAIRS[group]
    B, _, Lp, _ = qkv.shape
    L = Lp * ROW_PACK
    assert L <= ATTN_ROWS
    n_res = ATTN_ROWS // L

    def part(c):
        return pl.BlockSpec((None, n_res, Lp, D_MODEL), lambda b, r: (b, r, 0, c))

    return pl.pallas_call(
        functools.partial(_attn_kernel, dil=dil, slopes=_alibi_slopes(group)),
        name=f"attn_dil{dil}",
        grid=(B, dil // n_res),
        in_specs=[part(0), part(1), part(2)],
        out_specs=[
            pl.BlockSpec((None, n_res, Lp, D_MODEL), lambda b, r: (b, r, 0, 0)),
            pl.BlockSpec((None, n_res, L, LSE_LANES), lambda b, r: (b, r, 0, 0)),
        ],
        out_shape=[
            jax.ShapeDtypeStruct((B, dil, Lp, D_MODEL), jnp.int32),
            jax.ShapeDtypeStruct((B, dil, L, LSE_LANES), F32),
        ],
        scratch_shapes=[pltpu.VMEM((HEADS, Q_BLOCK, 2 * Q_BLOCK), F32)],
        compiler_params=pltpu.CompilerParams(
            dimension_semantics=("arbitrary", "arbitrary"), vmem_limit_bytes=VMEM_LIMIT_BYTES),
    )(qkv, qkv, qkv)


def _post_attn_kernel(x_ref, o1_ref, o2_ref, o3_ref, l1_ref, l2_ref, l3_ref, wo_hbm,
                      gf_ref, wgu_hbm, wd_hbm, out_ref, o_scr, l_scr, x1_scr, h_scr,
                      wo_ref, wgu_ref, wd_ref, stage_d, stage_f, sem, *, layer):
    tm = x_ref.shape[0]

    @pl.when(pl.program_id(0) == 0)
    def _():
        _fetch_weight_bf16(wo_hbm.at[0], wo_ref, stage_d, sem)
        _fetch_weight_bf16(wgu_hbm.at[layer], wgu_ref, stage_f, sem)
        _fetch_weight_bf16(wd_hbm.at[layer], wd_ref, stage_d, sem)
        x1_scr[...] = jnp.zeros_like(x1_scr)
        h_scr[...] = jnp.zeros_like(h_scr)

    out_ref[...] = x1_scr[...] + _dot(_swiglu_act(h_scr[...], wgu_ref), wd_ref[...])

    o_refs = (o1_ref, o2_ref, o3_ref)
    l_refs = (l1_ref, l2_ref, l3_ref)
    for g, (_, dil) in enumerate(ATTN_PAIRS):
        rows = tm // dil
        for r in range(dil):
            l_scr[g, pl.ds(r, rows, stride=dil), :] = l_refs[g][r]
            for h in range(HEADS):
                o_scr[g, h, pl.ds(r, rows, stride=dil), :] = (
                    _unpack_rows(o_refs[g][r, :, h * HEAD_DIM:(h + 1) * HEAD_DIM]).astype(F32))
    lse = [l_scr[g] for g in range(N_GROUPS)]
    m = jnp.maximum(jnp.maximum(lse[0], lse[1]), lse[2])
    e = [jnp.exp(l - m) for l in lse]
    inv = 1.0 / (e[0] + e[1] + e[2])
    wts = [ei * inv for ei in e]
    heads = []
    for h in range(HEADS):
        acc = wts[0][:, h:h + 1] * o_scr[0, h]
        for g in range(1, N_GROUPS):
            acc = acc + wts[g][:, h:h + 1] * o_scr[g, h]
        heads.append(acc.astype(BF16))
    merged = jnp.concatenate(heads, axis=-1)
    x1 = x_ref[...] + _dot(merged, wo_ref[...])
    x1_scr[...] = x1
    h_scr[...] = _rms(x1, gf_ref[...]).astype(BF16)


def _skewed_tiles(n_tiles, tiles_per_seq):
    def in_map(s):
        t = jnp.minimum(s, n_tiles - 1)
        return t // tiles_per_seq, t % tiles_per_seq

    def out_map(s):
        t = jnp.maximum(s - 1, 0)
        return t // tiles_per_seq, t % tiles_per_seq

    return in_map, out_map


def _post_attn(x, outs, lses, w_out, g_ffn, w_gu, w_down, layer):
    B, S, D = x.shape
    tm = POST_TM
    n_tiles = B * S // tm
    in_map, out_map = _skewed_tiles(n_tiles, S // tm)

    def folded(s):
        b, i = in_map(s)
        return b, 0, i, 0

    o_specs = [pl.BlockSpec((None, dil, tm // dil // ROW_PACK, D), folded) for _, dil in ATTN_PAIRS]
    l_specs = [pl.BlockSpec((None, dil, tm // dil, LSE_LANES), folded) for _, dil in ATTN_PAIRS]
    return pl.pallas_call(
        functools.partial(_post_attn_kernel, layer=layer),
        name="post_attn",
        grid=(n_tiles + 1,),
        in_specs=[pl.BlockSpec((None, tm, D), lambda s: (*in_map(s), 0)),
                  *o_specs, *l_specs, HBM_OPERAND, _resident((1, D)), HBM_OPERAND, HBM_OPERAND],
        out_specs=pl.BlockSpec((None, tm, D), lambda s: (*out_map(s), 0)),
        out_shape=jax.ShapeDtypeStruct((B, S, D), F32),
        scratch_shapes=[pltpu.VMEM((N_GROUPS, HEADS, tm, HEAD_DIM), F32),
                        pltpu.VMEM((N_GROUPS, tm, LSE_LANES), F32),
                        pltpu.VMEM((tm, D), F32), pltpu.VMEM((tm, D), BF16),
                        pltpu.VMEM((D, D), BF16), pltpu.VMEM((D, 2 * D_FF), BF16),
                        pltpu.VMEM((D_FF, D), BF16),
                        _stage_scratch(D_FF, D), _stage_scratch(D, 2 * D_FF), STAGE_SEMAPHORES],
        compiler_params=pltpu.CompilerParams(
            dimension_semantics=("arbitrary",), vmem_limit_bytes=VMEM_LIMIT_BYTES),
    )(x, *outs, *lses, w_out, g_ffn, w_gu, w_down)


def _pool_ffn_kernel(x_ref, gp_ref, win_hbm, wg_hbm, sc_ref, gf_ref, wgu_hbm, wd_hbm, gn_ref,
                     out_ref, u_scr, x2_scr, h_scr, win_ref, wg_ref, wgu_ref, wd_ref,
                     stage_d, stage_f, stage_g, sem,
                     *, n_tiles, tiles_per_seq, layer):
    tm = x_ref.shape[0]
    s = pl.program_id(0)
    i = jnp.minimum(s, n_tiles - 1) % tiles_per_seq

    @pl.when(s == 0)
    def _():
        _fetch_weight_bf16(win_hbm.at[0], win_ref, stage_d, sem)
        for g in range(len(POOL_WINDOWS)):
            _fetch_weight_bf16(wg_hbm.at[0, g], wg_ref.at[g], stage_g, sem)
        _fetch_weight_bf16(wgu_hbm.at[layer], wgu_ref, stage_f, sem)
        _fetch_weight_bf16(wd_hbm.at[layer], wd_ref, stage_d, sem)
        u_scr[...] = jnp.zeros_like(u_scr)
        x2_scr[...] = jnp.zeros_like(x2_scr)
        h_scr[...] = jnp.zeros_like(h_scr)

    x = x_ref[...]
    u = _dot(_rms(x, gp_ref[...]).astype(BF16), win_ref[...])
    act = _swiglu_act(h_scr[...], wgu_ref)
    prev_tail = u_scr[tm:tm + POOL_HALO, :]
    u_scr[0:POOL_HALO, :] = jnp.where(i == 0, 0.0, prev_tail)
    u_scr[POOL_HALO:, :] = u
    pos = i * tm + lax.broadcasted_iota(jnp.int32, (tm, 1), 0) + 1
    ys = []
    for g, w in enumerate(POOL_WINDOWS):
        cols = slice(g * POOL_GROUP_DIM, (g + 1) * POOL_GROUP_DIM)
        wsum = u_scr[:, cols]
        shift = 1
        while shift < w:
            wsum = wsum + pltpu.roll(wsum, shift, axis=0)
            shift *= 2
        count = jnp.minimum(pos, w).astype(F32)
        y = wsum[POOL_HALO:, :] / count - u[:, cols]
        ys.append(_dot(y.astype(BF16), wg_ref[g]))
    y = jnp.concatenate(ys, axis=-1) * sc_ref[...]
    x2 = x + y
    x3 = x2_scr[...] + _dot(act, wd_ref[...])
    out_ref[...] = _rms(x3, gn_ref[...])
    x2_scr[...] = x2
    h_scr[...] = _rms(x2, gf_ref[...]).astype(BF16)


def _pool_ffn(x, g_pool, w_in, w_group, scale, g_ffn, w_gu, w_down, g_final, layer):
    B, S, D = x.shape
    tm = POOL_TM
    n_tiles = B * S // tm
    in_map, out_map = _skewed_tiles(n_tiles, S // tm)
    return pl.pallas_call(
        functools.partial(_pool_ffn_kernel, n_tiles=n_tiles, tiles_per_seq=S // tm, layer=layer),
        name="pool_ffn",
        grid=(n_tiles + 1,),
        in_specs=[pl.BlockSpec((None, tm, D), lambda s: (*in_map(s), 0)),
                  _resident((1, D)), HBM_OPERAND, HBM_OPERAND,
                  _resident((1, D)), _resident((1, D)),
                  HBM_OPERAND, HBM_OPERAND, _resident((1, D))],
        out_specs=pl.BlockSpec((None, tm, D), lambda s: (*out_map(s), 0)),
        out_shape=jax.ShapeDtypeStruct((B, S, D), F32),
        scratch_shapes=[pltpu.VMEM((POOL_HALO + tm, D), F32),
                        pltpu.VMEM((tm, D), F32), pltpu.VMEM((tm, D), BF16),
                        pltpu.VMEM((D, D), BF16),
                        pltpu.VMEM((len(POOL_WINDOWS), POOL_GROUP_DIM, POOL_GROUP_DIM), BF16),
                        pltpu.VMEM((D, 2 * D_FF), BF16), pltpu.VMEM((D_FF, D), BF16),
                        _stage_scratch(D_FF, D), _stage_scratch(D, 2 * D_FF),
                        _stage_scratch(POOL_GROUP_DIM, POOL_GROUP_DIM), STAGE_SEMAPHORES],
        compiler_params=pltpu.CompilerParams(
            dimension_semantics=("arbitrary",), vmem_limit_bytes=VMEM_LIMIT_BYTES),
    )(x, g_pool, w_in, w_group, scale, g_ffn, w_gu, w_down, g_final)


def kernel(x, attn_norm, w_qkv, w_attn_out, pool_norm, w_pool_in, w_pool_group, pool_scale,
           ffn_norm, w_ffn_gate_up, w_ffn_down, final_norm):
    assert attn_norm.shape[0] == 1 and pool_norm.shape[0] == 1 and ffn_norm.shape[0] == 2
    row = lambda v: v.reshape(1, D_MODEL)

    outs, lses, qkv_far = _qkv_attn(x, row(attn_norm[0]), w_qkv)
    o_far, l_far = _attention(qkv_far, N_GROUPS - 1)
    x1 = _post_attn(x, (*outs, o_far), (*lses, l_far), w_attn_out, row(ffn_norm[0]),
                    w_ffn_gate_up, w_ffn_down, layer=0)
    return _pool_ffn(x1, row(pool_norm[0]), w_pool_in, w_pool_group, row(pool_scale[0]),
                     row(ffn_norm[1]), w_ffn_gate_up, w_ffn_down, row(final_norm), layer=1)
```
